```python
import jax, jax.numpy as jnp
from jax import lax
import numpy as np

D_MODEL = 1024
BATCH = 4
SEQ = 8192
DEPTH = 1

CONV_WIDTH = D_MODEL
CONV_K = 3
HEAD_DIM = 64
N_Q_HEADS = 8
N_KV_HEADS = 2
GQA_GROUP = N_Q_HEADS // N_KV_HEADS
WINDOW = 128
BLOCK = WINDOW
ROPE_THETA = 10000.0
Q_WIDTH = N_Q_HEADS * HEAD_DIM
KV_WIDTH = N_KV_HEADS * HEAD_DIM
D_FF = 2816
FFN_CONV_K = 3
EPS = 1e-5

IN_SIZES = (CONV_WIDTH, CONV_WIDTH, CONV_WIDTH, Q_WIDTH, KV_WIDTH, KV_WIDTH, D_MODEL, D_MODEL)
IN_WIDTH = sum(IN_SIZES)
IN_SPLITS = tuple(int(c) for c in np.cumsum(IN_SIZES)[:-1])

kernel_name = "hybrid_shortconv_swa_sinks_convffn"


def rmsnorm(x, w):
    xf = x.astype(jnp.float32)
    xf = xf * lax.rsqrt(jnp.mean(xf * xf, axis=-1, keepdims=True) + EPS)
    return xf.astype(x.dtype) * w


def causal_dwconv(v, w, b=None):
    k_taps = w.shape[0]
    s = v.shape[1]
    vp = jnp.pad(v, ((0, 0), (k_taps - 1, 0), (0, 0)))
    y = sum(w[i] * vp[:, i:i + s] for i in range(k_taps))
    if b is not None:
        y = y + b
    return y


def rope_tables(positions, dtype):
    inv_freq = ROPE_THETA ** (-jnp.arange(0, HEAD_DIM, 2, dtype=jnp.float32) / HEAD_DIM)
    ang = positions.astype(jnp.float32)[..., None] * inv_freq
    return jnp.cos(ang)[:, :, None, :].astype(dtype), jnp.sin(ang)[:, :, None, :].astype(dtype)


def apply_rope(x, cos, sin):
    x1, x2 = jnp.split(x, 2, axis=-1)
    return jnp.concatenate([x1 * cos - x2 * sin, x2 * cos + x1 * sin], axis=-1)


def sliding_window_attention(q, k, v, sinks):
    b, s = q.shape[:2]
    nb = s // BLOCK
    qb = q.reshape(b, nb, BLOCK, N_KV_HEADS, GQA_GROUP, HEAD_DIM)

    def with_prev(t):
        tb = t.reshape(b, nb, BLOCK, N_KV_HEADS, HEAD_DIM)
        prev = jnp.pad(tb[:, :-1], ((0, 0), (1, 0), (0, 0), (0, 0), (0, 0)))
        return jnp.concatenate([prev, tb], axis=2)

    kk, vv = with_prev(k), with_prev(v)
    scale = HEAD_DIM ** -0.5
    scores = jnp.einsum('bnqhgd,bnkhd->bnhgqk', qb, kk).astype(jnp.float32) * scale

    q_pos = jnp.arange(BLOCK)[:, None] + BLOCK
    k_pos = jnp.arange(2 * BLOCK)[None, :]
    rel = q_pos - k_pos
    band = (rel >= 0) & (rel < WINDOW)
    not_pad = (jnp.arange(nb)[:, None, None] > 0) | (k_pos[None] >= BLOCK)
    valid = band[None] & not_pad
    scores = jnp.where(valid[None, :, None, None], scores, -jnp.inf)

    sink = sinks.astype(jnp.float32).reshape(N_KV_HEADS, GQA_GROUP)[None, None, :, :, None, None]
    m = jnp.maximum(jnp.max(scores, axis=-1, keepdims=True), sink)
    p = jnp.exp(scores - m)
    denom = jnp.sum(p, axis=-1, keepdims=True) + jnp.exp(sink - m)
    probs = (p / denom).astype(v.dtype)
    out = jnp.einsum('bnhgqk,bnkhd->bnqhgd', probs, vv)
    return out.reshape(b, s, Q_WIDTH)


def setup_inputs(seed: int = 0) -> dict:
    key = jax.random.key(seed)
    ks = jax.random.split(key, 20)
    f32 = jnp.float32

    def nrm(k, shape, scale):
        return jax.random.normal(k, shape, f32) * scale

    x = jax.random.normal(ks[0], (BATCH, SEQ, D_MODEL), f32)
    positions = jnp.broadcast_to(jnp.arange(SEQ, dtype=jnp.int32), (BATCH, SEQ))
    return {
        "x": x,
        "positions": positions,
        "norm_mix_w": 1.0 + nrm(ks[1], (DEPTH, D_MODEL), 0.02),
        "w_in": nrm(ks[2], (DEPTH, D_MODEL, IN_WIDTH), D_MODEL ** -0.5),
        "b_in": nrm(ks[3], (DEPTH, IN_WIDTH), 0.02),
        "conv_mix_w": nrm(ks[4], (DEPTH, CONV_K, CONV_WIDTH), CONV_K ** -0.5),
        "w_conv_out": nrm(ks[5], (DEPTH, CONV_WIDTH, D_MODEL), CONV_WIDTH ** -0.5),
        "w_attn_out": nrm(ks[6], (DEPTH, Q_WIDTH, D_MODEL), Q_WIDTH ** -0.5),
        "b_attn_out": nrm(ks[7], (DEPTH, D_MODEL), 0.02),
        "sinks": nrm(ks[8], (DEPTH, N_Q_HEADS), 1.0),
        "w_mix_out": nrm(ks[9], (DEPTH, D_MODEL, D_MODEL), D_MODEL ** -0.5),
        "norm_ffn_w": 1.0 + nrm(ks[10], (DEPTH, D_MODEL), 0.02),
        "w_ffn_up": nrm(ks[11], (DEPTH, D_MODEL, 2 * D_FF), D_MODEL ** -0.5),
        "ffn_conv_w": nrm(ks[12], (DEPTH, FFN_CONV_K, D_FF), FFN_CONV_K ** -0.5),
        "ffn_conv_b": nrm(ks[13], (DEPTH, D_FF), 0.02),
        "w_ffn_down": nrm(ks[14], (DEPTH, D_FF, D_MODEL), D_FF ** -0.5),
        "norm_final_w": 1.0 + nrm(ks[15], (D_MODEL,), 0.02),
    }


def reference(x, positions, norm_mix_w, w_in, b_in, conv_mix_w, w_conv_out, w_attn_out,
              b_attn_out, sinks, w_mix_out, norm_ffn_w, w_ffn_up, ffn_conv_w, ffn_conv_b,
              w_ffn_down, norm_final_w):
    b, s, _ = x.shape
    cos, sin = rope_tables(positions, x.dtype)
    for l in range(DEPTH):
        u = rmsnorm(x, norm_mix_w[l])
        proj = u @ w_in[l] + b_in[l]
        cb, cc, cx, q, k, v, g_conv, g_attn = jnp.split(proj, IN_SPLITS, axis=-1)

        y_conv = (cb * causal_dwconv(cc * cx, conv_mix_w[l])) @ w_conv_out[l]

        q = apply_rope(q.reshape(b, s, N_Q_HEADS, HEAD_DIM), cos, sin)
        k = apply_rope(k.reshape(b, s, N_KV_HEADS, HEAD_DIM), cos, sin)
        v = v.reshape(b, s, N_KV_HEADS, HEAD_DIM)
        y_attn = sliding_window_attention(q, k, v, sinks[l]) @ w_attn_out[l] + b_attn_out[l]

        merged = jax.nn.sigmoid(g_conv) * y_conv + jax.nn.sigmoid(g_attn) * y_attn
        x = x + merged @ w_mix_out[l]

        u = rmsnorm(x, norm_ffn_w[l])
        up, gate = jnp.split(u @ w_ffn_up[l], 2, axis=-1)
        a = causal_dwconv(up, ffn_conv_w[l], ffn_conv_b[l])
        x = x + (jax.nn.silu(a) * gate) @ w_ffn_down[l]
    return rmsnorm(x, norm_final_w)
```

```python
import functools

import jax
import jax.numpy as jnp
from jax import lax
from jax.experimental import pallas as pl
from jax.experimental.pallas import tpu as pltpu

D_MODEL = 1024
HEAD_DIM = 64
N_Q_HEADS = 8
N_KV_HEADS = 2
GQA_GROUP = N_Q_HEADS // N_KV_HEADS
BLOCK = 128
ROPE_THETA = 10000.0
Q_WIDTH = N_Q_HEADS * HEAD_DIM
KV_WIDTH = N_KV_HEADS * HEAD_DIM
D_FF = 2816
EPS = 1e-5

OFF_CB = 0
OFF_CC = OFF_CB + D_MODEL
OFF_CX = OFF_CC + D_MODEL
OFF_Q = OFF_CX + D_MODEL
OFF_K = OFF_Q + Q_WIDTH
OFF_V = OFF_K + KV_WIDTH
OFF_GC = OFF_V + KV_WIDTH
OFF_GA = OFF_GC + D_MODEL
IN_WIDTH = OFF_GA + D_MODEL

LANES = 128
HALO = 8
TOKEN_TILE = 256
FF_CHUNKS = ((0, 1024), (1024, 2048), (2048, D_FF))
VMEM_LIMIT_BYTES = 56 * 1024 * 1024

F32 = jnp.float32
BF16 = jnp.bfloat16


def _rmsnorm(x, w):
    ms = jnp.mean(x * x, axis=-1, keepdims=True)
    return x * lax.rsqrt(ms + EPS) * w


def _dot(a, b):
    return jnp.dot(a, b, preferred_element_type=F32)


def _dot_nt(a, b):
    return lax.dot_general(a, b, (((1,), (1,)), ((), ())), preferred_element_type=F32)


def _causal_conv3(buf, cur, taps_ref, tm, c0, c1):
    return (taps_ref[0:1, c0:c1] * buf[HALO - 2:HALO - 2 + tm, c0:c1]
            + taps_ref[1:2, c0:c1] * buf[HALO - 1:HALO - 1 + tm, c0:c1]
            + taps_ref[2:3, c0:c1] * cur)


def _mixer_kernel(sinks_ref, pos_ref, x_ref, invf_ref, nw_ref, win_ref, bin_ref, cw_ref,
                  wco_ref, wao_ref, bao_ref, wmo_ref, o_ref,
                  zbuf, kbuf, vbuf, qbuf, abuf):
    j = pl.program_id(1)
    tm = x_ref.shape[1]

    @pl.when(j == 0)
    def _():
        zbuf[0:HALO, :] = jnp.zeros((HALO, D_MODEL), F32)
        kbuf[:, 0:BLOCK, :] = jnp.zeros((4, BLOCK, LANES), BF16)
        vbuf[:, 0:BLOCK, :] = jnp.zeros((4, BLOCK, LANES), BF16)

    x = x_ref[0]
    u = _rmsnorm(x, nw_ref[...]).astype(BF16)

    def proj(off, width):
        return _dot(u, win_ref[:, off:off + width]) + bin_ref[:, off:off + width]

    z = proj(OFF_CC, D_MODEL) * proj(OFF_CX, D_MODEL)
    zbuf[HALO:HALO + tm, :] = z
    conv = _causal_conv3(zbuf, z, cw_ref, tm, 0, D_MODEL)
    y_conv = _dot((proj(OFF_CB, D_MODEL) * conv).astype(BF16), wco_ref[...])
    zbuf[HALO - 2:HALO, :] = zbuf[HALO - 2 + tm:HALO + tm, :]

    ang = invf_ref[...] * pos_ref[0].astype(F32)
    cos_t, sin_t = jnp.cos(ang), jnp.sin(ang)
    cos = jnp.concatenate([cos_t] * 4, axis=0).T
    sin = jnp.concatenate([-sin_t, sin_t, -sin_t, sin_t], axis=0).T

    lane = lax.broadcasted_iota(jnp.int32, (tm, LANES), 1)
    first_half = (lane & (HEAD_DIM // 2)) == 0
    low_head = lane < HEAD_DIM

    def rope(v, c, s):
        partner = jnp.where(first_half, pltpu.roll(v, LANES - HEAD_DIM // 2, 1),
                            pltpu.roll(v, HEAD_DIM // 2, 1))
        return v * c + partner * s

    scale = HEAD_DIM ** -0.5
    cos_q, sin_q = cos * scale, sin * scale
    q = proj(OFF_Q, Q_WIDTH)
    for c in range(Q_WIDTH // LANES):
        sl = slice(c * LANES, (c + 1) * LANES)
        qbuf[:, sl] = rope(q[:, sl], cos_q, sin_q).astype(BF16)

    def scatter_heads(buf, v):
        vrot = pltpu.roll(v, HEAD_DIM, 1)
        zero = jnp.zeros_like(v)
        rows = slice(BLOCK, BLOCK + tm)
        buf[0, rows, :] = jnp.where(low_head, v, zero).astype(BF16)
        buf[1, rows, :] = jnp.where(low_head, zero, vrot).astype(BF16)
        buf[2, rows, :] = jnp.where(low_head, vrot, zero).astype(BF16)
        buf[3, rows, :] = jnp.where(low_head, zero, v).astype(BF16)

    scatter_heads(kbuf, rope(proj(OFF_K, KV_WIDTH), cos, sin))
    scatter_heads(vbuf, proj(OFF_V, KV_WIDTH))

    rows2 = lax.broadcasted_iota(jnp.int32, (2 * BLOCK, 2 * BLOCK), 0) & (BLOCK - 1)
    cols2 = lax.broadcasted_iota(jnp.int32, (2 * BLOCK, 2 * BLOCK), 1)
    band = (cols2 > rows2) & (cols2 <= rows2 + BLOCK)
    neg_inf = jnp.full((2 * BLOCK, 2 * BLOCK), -jnp.inf, F32)
    zeros2 = jnp.zeros((2 * BLOCK, 2 * BLOCK), F32)
    bias = jnp.where(band, zeros2, neg_inf)
    bias_seq_start = jnp.where(band & (cols2 >= BLOCK), zeros2, neg_inf)
    bias_first = jnp.where(j > 0, bias, bias_seq_start)

    lane2 = lax.broadcasted_iota(jnp.int32, (2 * BLOCK, LANES), 1)
    low2 = lane2 < HEAD_DIM
    ones_even = jnp.where(low2, 1.0, 0.0).astype(BF16)
    ones_odd = jnp.where(low2, 0.0, 1.0).astype(BF16)

    def sink_column(h_top, h_bottom):
        return jnp.concatenate([jnp.full((BLOCK, 1), sinks_ref[h_top], F32),
                                jnp.full((BLOCK, 1), sinks_ref[h_bottom], F32)], axis=0)

    for i in range(tm // BLOCK):
        qrows = slice(i * BLOCK, (i + 1) * BLOCK)
        krows = slice(i * BLOCK, (i + 2) * BLOCK)
        blk_bias = bias_first if i == 0 else bias
        for g in range(N_KV_HEADS):
            c0 = g * 2 * LANES
            lhs = jnp.concatenate([qbuf[qrows, c0:c0 + LANES],
                                   qbuf[qrows, c0 + LANES:c0 + 2 * LANES]], axis=0)
            s_even = _dot_nt(lhs, kbuf[2 * g, krows, :]) + blk_bias
            s_odd = _dot_nt(lhs, kbuf[2 * g + 1, krows, :]) + blk_bias
            sink_even = sink_column(4 * g, 4 * g + 2)
            sink_odd = sink_column(4 * g + 1, 4 * g + 3)
            m_even = jnp.maximum(jnp.max(s_even, axis=-1, keepdims=True), sink_even)
            m_odd = jnp.maximum(jnp.max(s_odd, axis=-1, keepdims=True), sink_odd)
            p = jnp.concatenate([jnp.exp(s_even - m_even).astype(BF16),
                                 jnp.exp(s_odd - m_odd).astype(BF16)], axis=1)
            rhs = jnp.concatenate(
                [jnp.concatenate([vbuf[2 * g, krows, :], ones_even], axis=1),
                 jnp.concatenate([vbuf[2 * g + 1, krows, :], ones_odd], axis=1)], axis=0)
            r = _dot(p, rhs)
            sink_term = jnp.where(low2, jnp.exp(sink_even - m_even), jnp.exp(sink_odd - m_odd))
            out = (r[:, :LANES] / (r[:, LANES:] + sink_term)).astype(BF16)
            abuf[qrows, c0:c0 + LANES] = out[:BLOCK]
            abuf[qrows, c0 + LANES:c0 + 2 * LANES] = out[BLOCK:]

    kbuf[:, 0:BLOCK, :] = kbuf[:, tm:tm + BLOCK, :]
    vbuf[:, 0:BLOCK, :] = vbuf[:, tm:tm + BLOCK, :]

    y_attn = _dot(abuf[...], wao_ref[...]) + bao_ref[...]

    merged = (jax.nn.sigmoid(proj(OFF_GC, D_MODEL)) * y_conv
              + jax.nn.sigmoid(proj(OFF_GA, D_MODEL)) * y_attn)
    o_ref[0] = x + _dot(merged.astype(BF16), wmo_ref[...])


def _ffn_kernel(final_norm, x_ref, nw_ref, wup_ref, cw_ref, cb_ref, wdn_ref, fw_ref, o_ref,
                ubuf, hbuf):
    j = pl.program_id(1)
    tm = x_ref.shape[1]

    @pl.when(j == 0)
    def _():
        ubuf[0:HALO, :] = jnp.zeros((HALO, D_FF), F32)

    x = x_ref[0]
    u = _rmsnorm(x, nw_ref[...]).astype(BF16)
    for c0, c1 in FF_CHUNKS:
        up = _dot(u, wup_ref[:, c0:c1])
        ubuf[HALO:HALO + tm, c0:c1] = up
        a = _causal_conv3(ubuf, up, cw_ref, tm, c0, c1) + cb_ref[:, c0:c1]
        gate = _dot(u, wup_ref[:, D_FF + c0:D_FF + c1])
        hbuf[:, c0:c1] = (jax.nn.silu(a) * gate).astype(BF16)
    ubuf[HALO - 2:HALO, :] = ubuf[HALO - 2 + tm:HALO + tm, :]
    y = x + _dot(hbuf[...], wdn_ref[...])
    o_ref[0] = _rmsnorm(y, fw_ref[...]) if final_norm else y


def _whole(space=pltpu.VMEM):
    return pl.BlockSpec(memory_space=space)


def kernel(x, positions, norm_mix_w, w_in, b_in, conv_mix_w, w_conv_out, w_attn_out, b_attn_out,
           sinks, w_mix_out, norm_ffn_w, w_ffn_up, ffn_conv_w, ffn_conv_b, w_ffn_down, norm_final_w):
    b, s, d = x.shape
    depth = w_in.shape[0]
    tm = TOKEN_TILE
    assert d == D_MODEL and s % tm == 0 and tm % BLOCK == 0
    nj = s // tm
    grid = (b, nj)
    tile_spec = pl.BlockSpec((1, tm, D_MODEL), lambda bi, ji: (bi, ji, 0))
    params = pltpu.CompilerParams(dimension_semantics=("arbitrary", "arbitrary"),
                                  vmem_limit_bytes=VMEM_LIMIT_BYTES)

    inv_freq = (ROPE_THETA ** (-jnp.arange(0, HEAD_DIM, 2, dtype=F32) / HEAD_DIM)).reshape(-1, 1)
    pos_tiles = positions.reshape(b * nj, 1, tm)
    pos_spec = pl.BlockSpec((1, 1, tm), lambda bi, ji: (bi * nj + ji, 0, 0))

    mixer = pl.pallas_call(
        _mixer_kernel,
        grid=grid,
        in_specs=[_whole(pltpu.SMEM), pos_spec, tile_spec] + [_whole()] * 9,
        out_specs=tile_spec,
        out_shape=jax.ShapeDtypeStruct((b, s, D_MODEL), F32),
        scratch_shapes=[
            pltpu.VMEM((HALO + tm, D_MODEL), F32),
            pltpu.VMEM((4, BLOCK + tm, LANES), BF16),
            pltpu.VMEM((4, BLOCK + tm, LANES), BF16),
            pltpu.VMEM((tm, Q_WIDTH), BF16),
            pltpu.VMEM((tm, Q_WIDTH), BF16),
        ],
        compiler_params=params,
        name="token_mixers",
    )
    def ffn(final_norm):
        return pl.pallas_call(
            functools.partial(_ffn_kernel, final_norm),
            grid=grid,
            in_specs=[tile_spec] + [_whole()] * 6,
            out_specs=tile_spec,
            out_shape=jax.ShapeDtypeStruct((b, s, D_MODEL), F32),
            scratch_shapes=[
                pltpu.VMEM((HALO + tm, D_FF), F32),
                pltpu.VMEM((tm, D_FF), BF16),
            ],
            compiler_params=params,
            name="channel_mixer",
        )

    row = lambda v: v.reshape(1, -1)
    for l in range(depth):
        x = mixer(sinks[l], pos_tiles, x, inv_freq, row(norm_mix_w[l]), w_in[l].astype(BF16),
                  row(b_in[l]), conv_mix_w[l], w_conv_out[l].astype(BF16),
                  w_attn_out[l].astype(BF16), row(b_attn_out[l]), w_mix_out[l].astype(BF16))
        x = ffn(l == depth - 1)(x, row(norm_ffn_w[l]), w_ffn_up[l].astype(BF16), ffn_conv_w[l],
                                row(ffn_conv_b[l]), w_ffn_down[l].astype(BF16),
                                row(norm_final_w))
    return x
```

```python
import functools

import jax
import jax.numpy as jnp
from jax import lax
from jax.experimental import pallas as pl
from jax.experimental.pallas import tpu as pltpu

D_MODEL = 1024
HEAD_DIM = 64
N_Q_HEADS = 8
N_KV_HEADS = 2
GQA_GROUP = N_Q_HEADS // N_KV_HEADS
BLOCK = 128
ROPE_THETA = 10000.0
Q_WIDTH = N_Q_HEADS * HEAD_DIM
KV_WIDTH = N_KV_HEADS * HEAD_DIM
D_FF = 2816
EPS = 1e-5

OFF_CB = 0
OFF_CC = OFF_CB + D_MODEL
OFF_CX = OFF_CC + D_MODEL
OFF_Q = OFF_CX + D_MODEL
OFF_K = OFF_Q + Q_WIDTH
OFF_V = OFF_K + KV_WIDTH
OFF_GC = OFF_V + KV_WIDTH
OFF_GA = OFF_GC + D_MODEL
IN_WIDTH = OFF_GA + D_MODEL

LANES = 128
HALO = 8
TOKEN_TILE = 256
FF_CHUNKS = ((0, 1024), (1024, 2048), (2048, D_FF))
VMEM_LIMIT_BYTES = 56 * 1024 * 1024

F32 = jnp.float32
BF16 = jnp.bfloat16


def _rmsnorm(x, w):
    ms = jnp.mean(x * x, axis=-1, keepdims=True)
    return x * lax.rsqrt(ms + EPS) * w


def _dot(a, b):
    return jnp.dot(a, b, preferred_element_type=F32)


def _dot_nt(a, b):
    return lax.dot_general(a, b, (((1,), (1,)), ((), ())), preferred_element_type=F32)


def _causal_conv3(buf, cur, taps_ref, tm, c0, c1):
    return (taps_ref[0:1, c0:c1] * buf[HALO - 2:HALO - 2 + tm, c0:c1]
            + taps_ref[1:2, c0:c1] * buf[HALO - 1:HALO - 1 + tm, c0:c1]
            + taps_ref[2:3, c0:c1] * cur)


def _mixer_kernel(sinks_ref, pos_ref, x_ref, invf_ref, nw_ref, win_ref, bin_ref, cw_ref,
                  wco_ref, wao_ref, bao_ref, wmo_ref, o_ref,
                  zbuf, kbuf, vbuf, qbuf, abuf, pbuf):
    j = pl.program_id(1)
    tm = x_ref.shape[1]

    @pl.when(j == 0)
    def _():
        zbuf[0:HALO, :] = jnp.zeros((HALO, D_MODEL), F32)
        kbuf[:, 0:BLOCK, :] = jnp.zeros((4, BLOCK, LANES), BF16)
        vbuf[:, 0:BLOCK, :] = jnp.zeros((4, BLOCK, LANES), BF16)

    x = x_ref[0]
    u = _rmsnorm(x, nw_ref[...]).astype(BF16)

    def proj(off, width):
        return _dot(u, win_ref[:, off:off + width]) + bin_ref[:, off:off + width]


    qkv = proj(OFF_Q, Q_WIDTH + 2 * KV_WIDTH)
    z = proj(OFF_CC, D_MODEL) * proj(OFF_CX, D_MODEL)

    ang = invf_ref[...] * pos_ref[0].astype(F32)
    cos_t, sin_t = jnp.cos(ang), jnp.sin(ang)
    cos = jnp.concatenate([cos_t] * 4, axis=0).T
    sin = jnp.concatenate([-sin_t, sin_t, -sin_t, sin_t], axis=0).T

    lane = lax.broadcasted_iota(jnp.int32, (tm, LANES), 1)
    first_half = (lane & (HEAD_DIM // 2)) == 0
    low_head = lane < HEAD_DIM

    def rope(v, c, s):
        partner = jnp.where(first_half, pltpu.roll(v, LANES - HEAD_DIM // 2, 1),
                            pltpu.roll(v, HEAD_DIM // 2, 1))
        return v * c + partner * s

    scale = HEAD_DIM ** -0.5
    cos_q, sin_q = cos * scale, sin * scale
    for c in range(Q_WIDTH // LANES):
        sl = slice(c * LANES, (c + 1) * LANES)
        qbuf[:, sl] = rope(qkv[:, sl], cos_q, sin_q).astype(BF16)

    def scatter_heads(buf, v):
        vrot = pltpu.roll(v, HEAD_DIM, 1)
        zero = jnp.zeros_like(v)
        rows = slice(BLOCK, BLOCK + tm)
        buf[0, rows, :] = jnp.where(low_head, v, zero).astype(BF16)
        buf[1, rows, :] = jnp.where(low_head, zero, vrot).astype(BF16)
        buf[2, rows, :] = jnp.where(low_head, vrot, zero).astype(BF16)
        buf[3, rows, :] = jnp.where(low_head, zero, v).astype(BF16)

    scatter_heads(kbuf, rope(qkv[:, Q_WIDTH:Q_WIDTH + KV_WIDTH], cos, sin))
    scatter_heads(vbuf, qkv[:, Q_WIDTH + KV_WIDTH:])

    rows2 = lax.broadcasted_iota(jnp.int32, (2 * BLOCK, 2 * BLOCK), 0) & (BLOCK - 1)
    cols2 = lax.broadcasted_iota(jnp.int32, (2 * BLOCK, 2 * BLOCK), 1)
    band = (cols2 > rows2) & (cols2 <= rows2 + BLOCK)
    neg_inf = jnp.full((2 * BLOCK, 2 * BLOCK), -jnp.inf, F32)
    zeros2 = jnp.zeros((2 * BLOCK, 2 * BLOCK), F32)
    bias = jnp.where(band, zeros2, neg_inf)
    bias_seq_start = jnp.where(band & (cols2 >= BLOCK), zeros2, neg_inf)
    bias_first = jnp.where(j > 0, bias, bias_seq_start)

    blocks = [(i, g) for i in range(tm // BLOCK) for g in range(N_KV_HEADS)]
    scores = {}
    for i, g in blocks:
        qrows = slice(i * BLOCK, (i + 1) * BLOCK)
        krows = slice(i * BLOCK, (i + 2) * BLOCK)
        blk_bias = bias_first if i == 0 else bias
        c0 = g * 2 * LANES
        lhs = jnp.concatenate([qbuf[qrows, c0:c0 + LANES],
                               qbuf[qrows, c0 + LANES:c0 + 2 * LANES]], axis=0)
        scores[i, g] = (_dot_nt(lhs, kbuf[2 * g, krows, :]) + blk_bias,
                        _dot_nt(lhs, kbuf[2 * g + 1, krows, :]) + blk_bias)

    zbuf[HALO:HALO + tm, :] = z
    conv = _causal_conv3(zbuf, z, cw_ref, tm, 0, D_MODEL)
    zbuf[HALO - 2:HALO, :] = zbuf[HALO - 2 + tm:HALO + tm, :]

    lane2 = lax.broadcasted_iota(jnp.int32, (2 * BLOCK, LANES), 1)
    low2 = lane2 < HEAD_DIM
    ones_even = jnp.where(low2, 1.0, 0.0).astype(BF16)
    ones_odd = jnp.where(low2, 0.0, 1.0).astype(BF16)

    def sink_column(h_top, h_bottom):
        return jnp.concatenate([jnp.full((BLOCK, 1), sinks_ref[h_top], F32),
                                jnp.full((BLOCK, 1), sinks_ref[h_bottom], F32)], axis=0)

    cb_width = D_MODEL // len(blocks)
    probs = {}
    gated = []
    for n, (i, g) in enumerate(blocks):
        cols = slice(n * cb_width, (n + 1) * cb_width)
        gated.append((proj(OFF_CB + n * cb_width, cb_width) * conv[:, cols]).astype(BF16))
        s_even, s_odd = scores[i, g]
        sink_even = sink_column(4 * g, 4 * g + 2)
        sink_odd = sink_column(4 * g + 1, 4 * g + 3)
        m_even = jnp.maximum(jnp.max(s_even, axis=-1, keepdims=True), sink_even)
        m_odd = jnp.maximum(jnp.max(s_odd, axis=-1, keepdims=True), sink_odd)
        pbuf[n, :, 0:2 * BLOCK] = jnp.exp(s_even - m_even).astype(BF16)
        pbuf[n, :, 2 * BLOCK:] = jnp.exp(s_odd - m_odd).astype(BF16)
        probs[i, g] = jnp.where(low2, jnp.exp(sink_even - m_even), jnp.exp(sink_odd - m_odd))
    y_conv = _dot(jnp.concatenate(gated, axis=1), wco_ref[...])

    unnormalised = {}
    for n, (i, g) in enumerate(blocks):
        krows = slice(i * BLOCK, (i + 2) * BLOCK)
        rhs = jnp.concatenate(
            [jnp.concatenate([vbuf[2 * g, krows, :], ones_even], axis=1),
             jnp.concatenate([vbuf[2 * g + 1, krows, :], ones_odd], axis=1)], axis=0)
        unnormalised[i, g] = (_dot(pbuf[n], rhs), probs[i, g])

    kbuf[:, 0:BLOCK, :] = kbuf[:, tm:tm + BLOCK, :]
    vbuf[:, 0:BLOCK, :] = vbuf[:, tm:tm + BLOCK, :]

    gate_conv = proj(OFF_GC, D_MODEL)
    gate_attn = proj(OFF_GA, D_MODEL)

    for i, g in blocks:
        qrows = slice(i * BLOCK, (i + 1) * BLOCK)
        c0 = g * 2 * LANES
        r, sink_term = unnormalised[i, g]
        out = (r[:, :LANES] / (r[:, LANES:] + sink_term)).astype(BF16)
        abuf[qrows, c0:c0 + LANES] = out[:BLOCK]
        abuf[qrows, c0 + LANES:c0 + 2 * LANES] = out[BLOCK:]
    y_attn = _dot(abuf[...], wao_ref[...]) + bao_ref[...]

    merged = jax.nn.sigmoid(gate_conv) * y_conv + jax.nn.sigmoid(gate_attn) * y_attn
    o_ref[0] = x + _dot(merged.astype(BF16), wmo_ref[...])


def _ffn_kernel(final_norm, x_ref, nw_ref, wup_ref, cw_ref, cb_ref, wdn_ref, fw_ref, o_ref,
                ubuf, hbuf):
    j = pl.program_id(1)
    tm = x_ref.shape[1]

    @pl.when(j == 0)
    def _():
        ubuf[0:HALO, :] = jnp.zeros((HALO, D_FF), F32)

    x = x_ref[0]
    u = _rmsnorm(x, nw_ref[...]).astype(BF16)
    for c0, c1 in FF_CHUNKS:
        up = _dot(u, wup_ref[:, c0:c1])
        ubuf[HALO:HALO + tm, c0:c1] = up
        a = _causal_conv3(ubuf, up, cw_ref, tm, c0, c1) + cb_ref[:, c0:c1]
        gate = _dot(u, wup_ref[:, D_FF + c0:D_FF + c1])
        hbuf[:, c0:c1] = (jax.nn.silu(a) * gate).astype(BF16)
    ubuf[HALO - 2:HALO, :] = ubuf[HALO - 2 + tm:HALO + tm, :]
    y = x + _dot(hbuf[...], wdn_ref[...])
    o_ref[0] = _rmsnorm(y, fw_ref[...]) if final_norm else y


def _whole(space=pltpu.VMEM):
    return pl.BlockSpec(memory_space=space)


def kernel(x, positions, norm_mix_w, w_in, b_in, conv_mix_w, w_conv_out, w_attn_out, b_attn_out,
           sinks, w_mix_out, norm_ffn_w, w_ffn_up, ffn_conv_w, ffn_conv_b, w_ffn_down, norm_final_w):
    b, s, d = x.shape
    depth = w_in.shape[0]
    tm = TOKEN_TILE
    assert d == D_MODEL and s % tm == 0 and tm % BLOCK == 0
    nj = s // tm
    grid = (b, nj)
    tile_spec = pl.BlockSpec((1, tm, D_MODEL), lambda bi, ji: (bi, ji, 0))
    params = pltpu.CompilerParams(dimension_semantics=("arbitrary", "arbitrary"),
                                  vmem_limit_bytes=VMEM_LIMIT_BYTES)

    inv_freq = (ROPE_THETA ** (-jnp.arange(0, HEAD_DIM, 2, dtype=F32) / HEAD_DIM)).reshape(-1, 1)
    pos_tiles = positions.reshape(b * nj, 1, tm)
    pos_spec = pl.BlockSpec((1, 1, tm), lambda bi, ji: (bi * nj + ji, 0, 0))

    mixer = pl.pallas_call(
        _mixer_kernel,
        grid=grid,
        in_specs=[_whole(pltpu.SMEM), pos_spec, tile_spec] + [_whole()] * 9,
        out_specs=tile_spec,
        out_shape=jax.ShapeDtypeStruct((b, s, D_MODEL), F32),
        scratch_shapes=[
            pltpu.VMEM((HALO + tm, D_MODEL), F32),
            pltpu.VMEM((4, BLOCK + tm, LANES), BF16),
            pltpu.VMEM((4, BLOCK + tm, LANES), BF16),
            pltpu.VMEM((tm, Q_WIDTH), BF16),
            pltpu.VMEM((tm, Q_WIDTH), BF16),
            pltpu.VMEM((tm // BLOCK * N_KV_HEADS, 2 * BLOCK, 4 * BLOCK), BF16),
        ],
        compiler_params=params,
        name="token_mixers",
    )

    def ffn(final_norm):
        return pl.pallas_call(
            functools.partial(_ffn_kernel, final_norm),
            grid=grid,
            in_specs=[tile_spec] + [_whole()] * 6,
            out_specs=tile_spec,
            out_shape=jax.ShapeDtypeStruct((b, s, D_MODEL), F32),
            scratch_shapes=[
                pltpu.VMEM((HALO + tm, D_FF), F32),
                pltpu.VMEM((tm, D_FF), BF16),
            ],
            compiler_params=params,
            name="channel_mixer",
        )

    row = lambda v: v.reshape(1, -1)
    for l in range(depth):
        x = mixer(sinks[l], pos_tiles, x, inv_freq, row(norm_mix_w[l]), w_in[l].astype(BF16),
                  row(b_in[l]), conv_mix_w[l], w_conv_out[l].astype(BF16),
                  w_attn_out[l].astype(BF16), row(b_attn_out[l]), w_mix_out[l].astype(BF16))
        x = ffn(l == depth - 1)(x, row(norm_ffn_w[l]), w_ffn_up[l].astype(BF16), ffn_conv_w[l],
                                row(ffn_conv_b[l]), w_ffn_down[l].astype(BF16),
                                row(norm_final_w))
    return x
```

```python
import functools

import jax
import jax.numpy as jnp
from jax import lax
from jax.experimental import pallas as pl
from jax.experimental.pallas import tpu as pltpu

D_MODEL = 1024
HEAD_DIM = 64
N_Q_HEADS = 8
N_KV_HEADS = 2
BLOCK = 128
ROPE_THETA = 10000.0
Q_WIDTH = N_Q_HEADS * HEAD_DIM
KV_WIDTH = N_KV_HEADS * HEAD_DIM
D_FF = 2816
EPS = 1e-5

OFF_CB = 0
OFF_CC = OFF_CB + D_MODEL
OFF_CX = OFF_CC + D_MODEL
OFF_Q = OFF_CX + D_MODEL
OFF_K = OFF_Q + Q_WIDTH
OFF_V = OFF_K + KV_WIDTH
OFF_GC = OFF_V + KV_WIDTH
OFF_GA = OFF_GC + D_MODEL

LANES = 128
HALO = 8
TOKEN_TILE = 256
FF_CHUNKS = ((0, 1024), (1024, 2048), (2048, D_FF))
VMEM_LIMIT_BYTES = 60 * 1024 * 1024

F32 = jnp.float32
BF16 = jnp.bfloat16


def _rmsnorm(x, w):
    ms = jnp.mean(x * x, axis=-1, keepdims=True)
    return x * lax.rsqrt(ms + EPS) * w


def _dot(a, b):
    return jnp.dot(a, b, preferred_element_type=F32)


def _dot_nt(a, b):
    return lax.dot_general(a, b, (((1,), (1,)), ((), ())), preferred_element_type=F32)


def _causal_conv3(buf, cur, taps_ref, tm, c0, c1):
    return (taps_ref[0:1, c0:c1] * buf[HALO - 2:HALO - 2 + tm, c0:c1]
            + taps_ref[1:2, c0:c1] * buf[HALO - 1:HALO - 1 + tm, c0:c1]
            + taps_ref[2:3, c0:c1] * cur)


def _block_kernel(tiles_per_seq, final_norm,
                  sinks_ref, pos_ref, x_ref, invf_ref, nmw_ref, win_ref, bin_ref, cmw_ref,
                  wco_ref, wao_ref, bao_ref, wmo_ref, nfw_ref, wup_ref, fcw_ref, fcb_ref,
                  wdn_ref, fin_ref, o_ref,
                  zbuf, kbuf, vbuf, qbuf, abuf, x1buf, u1buf, ubuf, hbuf):
    t = pl.program_id(0)
    tm = x_ref.shape[1]
    mixer_seq_start = (t % tiles_per_seq) == 0
    ffn_seq_start = ((t + tiles_per_seq - 1) % tiles_per_seq) == 0

    @pl.when(t == 0)
    def _():
        x1buf[...] = jnp.zeros(x1buf.shape, F32)
        u1buf[...] = jnp.zeros(u1buf.shape, BF16)

    @pl.when(mixer_seq_start)
    def _():
        zbuf[0:HALO, :] = jnp.zeros((HALO, D_MODEL), F32)
        kbuf[:, 0:BLOCK, :] = jnp.zeros((4, BLOCK, LANES), BF16)
        vbuf[:, 0:BLOCK] = jnp.zeros((KV_WIDTH, BLOCK), BF16)

    @pl.when(jnp.logical_or(ffn_seq_start, t == 0))
    def _():
        ubuf[0:HALO, :] = jnp.zeros((HALO, D_FF), F32)

    x_prev = x1buf[...]
    u_prev = u1buf[...]
    x = x_ref[0]
    u = _rmsnorm(x, nmw_ref[...]).astype(BF16)

    def proj(off, width):
        return _dot(u, win_ref[:, off:off + width]) + bin_ref[:, off:off + width]

    def ffn_up(c0, c1):
        up = _dot(u_prev, wup_ref[:, c0:c1])
        ubuf[HALO:HALO + tm, c0:c1] = up
        a = _causal_conv3(ubuf, up, fcw_ref, tm, c0, c1) + fcb_ref[:, c0:c1]
        gate = _dot(u_prev, wup_ref[:, D_FF + c0:D_FF + c1])
        hbuf[:, c0:c1] = (jax.nn.silu(a) * gate).astype(BF16)

    ffn_up(*FF_CHUNKS[0])
    qkv = proj(OFF_Q, Q_WIDTH + 2 * KV_WIDTH)
    ffn_up(*FF_CHUNKS[1])

    ang = invf_ref[...] * pos_ref[0].astype(F32)
    cos_t, sin_t = jnp.cos(ang), jnp.sin(ang)
    cos = jnp.concatenate([cos_t] * 4, axis=0).T
    sin = jnp.concatenate([-sin_t, sin_t, -sin_t, sin_t], axis=0).T

    lane = lax.broadcasted_iota(jnp.int32, (tm, LANES), 1)
    first_half = (lane & (HEAD_DIM // 2)) == 0
    low_head = lane < HEAD_DIM

    def rope(v, c, s):
        partner = jnp.where(first_half, pltpu.roll(v, LANES - HEAD_DIM // 2, 1),
                            pltpu.roll(v, HEAD_DIM // 2, 1))
        return v * c + partner * s

    scale = HEAD_DIM ** -0.5
    cos_q, sin_q = cos * scale, sin * scale
    for c in range(Q_WIDTH // LANES):
        sl = slice(c * LANES, (c + 1) * LANES)
        qbuf[:, sl] = rope(qkv[:, sl], cos_q, sin_q).astype(BF16)

    k = rope(qkv[:, Q_WIDTH:Q_WIDTH + KV_WIDTH], cos, sin)
    krot = pltpu.roll(k, HEAD_DIM, 1)
    kzero = jnp.zeros_like(k)
    rows = slice(BLOCK, BLOCK + tm)
    kbuf[0, rows, :] = jnp.where(low_head, k, kzero).astype(BF16)
    kbuf[1, rows, :] = jnp.where(low_head, kzero, krot).astype(BF16)
    kbuf[2, rows, :] = jnp.where(low_head, krot, kzero).astype(BF16)
    kbuf[3, rows, :] = jnp.where(low_head, kzero, k).astype(BF16)
    vbuf[:, BLOCK:BLOCK + tm] = qkv[:, Q_WIDTH + KV_WIDTH:].T.astype(BF16)

    key2 = lax.broadcasted_iota(jnp.int32, (2 * BLOCK, 2 * BLOCK), 0)
    qry2 = lax.broadcasted_iota(jnp.int32, (2 * BLOCK, 2 * BLOCK), 1) & (BLOCK - 1)
    band = (key2 > qry2) & (key2 <= qry2 + BLOCK)
    neg_inf = jnp.full((2 * BLOCK, 2 * BLOCK), -jnp.inf, F32)
    zeros2 = jnp.zeros((2 * BLOCK, 2 * BLOCK), F32)
    bias = jnp.where(band, zeros2, neg_inf)
    bias_seq_start = jnp.where(band & (key2 >= BLOCK), zeros2, neg_inf)
    bias_first = jnp.where(mixer_seq_start, bias_seq_start, bias)

    blocks = [(i, g) for i in range(tm // BLOCK) for g in range(N_KV_HEADS)]
    scores = {}
    for i, g in blocks:
        qrows = slice(i * BLOCK, (i + 1) * BLOCK)
        krows = slice(i * BLOCK, (i + 2) * BLOCK)
        blk_bias = bias_first if i == 0 else bias
        c0 = g * 2 * LANES
        qpair = jnp.concatenate([qbuf[qrows, c0:c0 + LANES],
                                 qbuf[qrows, c0 + LANES:c0 + 2 * LANES]], axis=0)
        scores[i, g] = (_dot_nt(kbuf[2 * g, krows, :], qpair) + blk_bias,
                        _dot_nt(kbuf[2 * g + 1, krows, :], qpair) + blk_bias)

    z = proj(OFF_CC, D_MODEL) * proj(OFF_CX, D_MODEL)

    top_pair = lax.broadcasted_iota(jnp.int32, (1, 2 * BLOCK), 1) < BLOCK
    sum_row = lax.broadcasted_iota(jnp.int32, (2 * HALO, 4 * BLOCK), 0)
    sum_col = lax.broadcasted_iota(jnp.int32, (2 * HALO, 4 * BLOCK), 1)
    sum_rows = jnp.where(sum_row == (sum_col >= 2 * BLOCK).astype(jnp.int32), 1.0, 0.0)
    sum_rows = sum_rows.astype(BF16)
    vzero = jnp.zeros((HEAD_DIM, 2 * BLOCK), BF16)

    for i, g in blocks:
        qrows = slice(i * BLOCK, (i + 1) * BLOCK)
        kcols = slice(i * BLOCK, (i + 2) * BLOCK)
        c0 = g * 2 * LANES
        s_even, s_odd = scores[i, g]
        sink_even = jnp.where(top_pair, sinks_ref[4 * g], sinks_ref[4 * g + 2])
        sink_odd = jnp.where(top_pair, sinks_ref[4 * g + 1], sinks_ref[4 * g + 3])
        m_even = jnp.maximum(jnp.max(s_even, axis=0, keepdims=True), sink_even)
        m_odd = jnp.maximum(jnp.max(s_odd, axis=0, keepdims=True), sink_odd)
        p_t = jnp.concatenate([jnp.exp(s_even - m_even).astype(BF16),
                               jnp.exp(s_odd - m_odd).astype(BF16)], axis=0)
        v_t = vbuf[g * HEAD_DIM:(g + 1) * HEAD_DIM, kcols]
        lhs = jnp.concatenate([jnp.concatenate([v_t, vzero], axis=1),
                               jnp.concatenate([vzero, v_t], axis=1), sum_rows], axis=0)
        r = _dot(lhs, p_t)
        inv_even = 1.0 / (r[2 * HEAD_DIM:2 * HEAD_DIM + 1] + jnp.exp(sink_even - m_even))
        inv_odd = 1.0 / (r[2 * HEAD_DIM + 1:2 * HEAD_DIM + 2] + jnp.exp(sink_odd - m_odd))
        out = jnp.concatenate([r[0:HEAD_DIM] * inv_even,
                               r[HEAD_DIM:2 * HEAD_DIM] * inv_odd], axis=0).T.astype(BF16)
        abuf[qrows, c0:c0 + LANES] = out[:BLOCK]
        abuf[qrows, c0 + LANES:c0 + 2 * LANES] = out[BLOCK:]

    kbuf[:, 0:BLOCK, :] = kbuf[:, tm:tm + BLOCK, :]
    vbuf[:, 0:BLOCK] = vbuf[:, tm:tm + BLOCK]

    ffn_up(*FF_CHUNKS[2])
    ubuf[HALO - 2:HALO, :] = ubuf[HALO - 2 + tm:HALO + tm, :]

    zbuf[HALO:HALO + tm, :] = z
    conv = _causal_conv3(zbuf, z, cmw_ref, tm, 0, D_MODEL)
    zbuf[HALO - 2:HALO, :] = zbuf[HALO - 2 + tm:HALO + tm, :]
    gated = (proj(OFF_CB, D_MODEL) * conv).astype(BF16)
    gate_conv = proj(OFF_GC, D_MODEL)
    gate_attn = proj(OFF_GA, D_MODEL)
    y_conv = _dot(gated, wco_ref[...])

    y_attn = _dot(abuf[...], wao_ref[...]) + bao_ref[...]

    y = x_prev + _dot(hbuf[...], wdn_ref[...])
    o_ref[0] = _rmsnorm(y, fin_ref[...]) if final_norm else y

    merged = jax.nn.sigmoid(gate_conv) * y_conv + jax.nn.sigmoid(gate_attn) * y_attn
    x1 = x + _dot(merged.astype(BF16), wmo_ref[...])
    x1buf[...] = x1
    u1buf[...] = _rmsnorm(x1, nfw_ref[...]).astype(BF16)


def _whole(space=pltpu.VMEM):
    return pl.BlockSpec(memory_space=space)


def kernel(x, positions, norm_mix_w, w_in, b_in, conv_mix_w, w_conv_out, w_attn_out, b_attn_out,
           sinks, w_mix_out, norm_ffn_w, w_ffn_up, ffn_conv_w, ffn_conv_b, w_ffn_down, norm_final_w):
    b, s, d = x.shape
    depth = w_in.shape[0]
    tm = TOKEN_TILE
    assert d == D_MODEL and s % tm == 0 and tm % BLOCK == 0
    nj = s // tm
    n_tiles = b * nj

    def in_tile(t):
        return jnp.minimum(t, n_tiles - 1)

    x_spec = pl.BlockSpec((1, tm, D_MODEL), lambda t: (in_tile(t) // nj, in_tile(t) % nj, 0))
    pos_spec = pl.BlockSpec((1, 1, tm), lambda t: (in_tile(t), 0, 0))
    out_spec = pl.BlockSpec((1, tm, D_MODEL),
                            lambda t: (jnp.maximum(t - 1, 0) // nj, jnp.maximum(t - 1, 0) % nj, 0))

    inv_freq = (ROPE_THETA ** (-jnp.arange(0, HEAD_DIM, 2, dtype=F32) / HEAD_DIM)).reshape(-1, 1)
    pos_tiles = positions.reshape(n_tiles, 1, tm)
    row = lambda v: v.reshape(1, -1)

    def layer(final_norm):
        return pl.pallas_call(
            functools.partial(_block_kernel, nj, final_norm),
            grid=(n_tiles + 1,),
            in_specs=[_whole(pltpu.SMEM), pos_spec, x_spec] + [_whole()] * 15,
            out_specs=out_spec,
            out_shape=jax.ShapeDtypeStruct((b, s, D_MODEL), F32),
            scratch_shapes=[
                pltpu.VMEM((HALO + tm, D_MODEL), F32),
                pltpu.VMEM((4, BLOCK + tm, LANES), BF16),
                pltpu.VMEM((KV_WIDTH, BLOCK + tm), BF16),
                pltpu.VMEM((tm, Q_WIDTH), BF16),
                pltpu.VMEM((tm, Q_WIDTH), BF16),
                pltpu.VMEM((tm, D_MODEL), F32),
                pltpu.VMEM((tm, D_MODEL), BF16),
                pltpu.VMEM((HALO + tm, D_FF), F32),
                pltpu.VMEM((tm, D_FF), BF16),
            ],
            compiler_params=pltpu.CompilerParams(dimension_semantics=("arbitrary",),
                                                 vmem_limit_bytes=VMEM_LIMIT_BYTES),
            name="decoder_block",
        )

    for l in range(depth):
        x = layer(l == depth - 1)(
            sinks[l], pos_tiles, x, inv_freq, row(norm_mix_w[l]), w_in[l].astype(BF16),
            row(b_in[l]), conv_mix_w[l], w_conv_out[l].astype(BF16), w_attn_out[l].astype(BF16),
            row(b_attn_out[l]), w_mix_out[l].astype(BF16), row(norm_ffn_w[l]),
            w_ffn_up[l].astype(BF16), ffn_conv_w[l], row(ffn_conv_b[l]),
            w_ffn_down[l].astype(BF16), row(norm_final_w))
    return x
```

```python
import functools

import jax
import jax.numpy as jnp
from jax import lax
from jax.experimental import pallas as pl
from jax.experimental.pallas import tpu as pltpu

D_MODEL = 1024
HEAD_DIM = 64
N_Q_HEADS = 8
N_KV_HEADS = 2
BLOCK = 128
ROPE_THETA = 10000.0
Q_WIDTH = N_Q_HEADS * HEAD_DIM
KV_WIDTH = N_KV_HEADS * HEAD_DIM
D_FF = 2816
EPS = 1e-5

OFF_CB = 0
OFF_CC = OFF_CB + D_MODEL
OFF_CX = OFF_CC + D_MODEL
OFF_Q = OFF_CX + D_MODEL
OFF_K = OFF_Q + Q_WIDTH
OFF_V = OFF_K + KV_WIDTH
OFF_GC = OFF_V + KV_WIDTH
OFF_GA = OFF_GC + D_MODEL
IN_WIDTH = OFF_GA + D_MODEL

ROW_B_IN, ROW_NORM_MIX, ROW_B_ATTN_OUT, ROW_NORM_FFN, ROW_NORM_FINAL = 0, 1, 2, 3, 4
ROW_CONV_MIX, ROW_FFN_CONV, ROW_FFN_CONV_B = 5, 8, 11
COL_W_IN = 0
COL_W_UP = COL_W_IN + IN_WIDTH
COL_W_CONV_OUT = COL_W_UP + 2 * D_FF
COL_W_MIX_OUT = COL_W_CONV_OUT + D_MODEL
COL_W_END = COL_W_MIX_OUT + D_MODEL

LANES = 128
HALO = 8
TOKEN_TILE = 256
FF_CHUNKS = ((0, 1024), (1024, 2048), (2048, D_FF))
VMEM_LIMIT_BYTES = 60 * 1024 * 1024

F32 = jnp.float32
BF16 = jnp.bfloat16


def _rmsnorm(x, w):
    ms = jnp.mean(x * x, axis=-1, keepdims=True)
    return x * lax.rsqrt(ms + EPS) * w


def _dot(a, b):
    return jnp.dot(a, b, preferred_element_type=F32)


def _dot_nt(a, b):
    return lax.dot_general(a, b, (((1,), (1,)), ((), ())), preferred_element_type=F32)


def _causal_conv3(buf, cur, taps_ref, tm, c0, c1):
    return (taps_ref[0:1, c0:c1] * buf[HALO - 2:HALO - 2 + tm, c0:c1]
            + taps_ref[1:2, c0:c1] * buf[HALO - 1:HALO - 1 + tm, c0:c1]
            + taps_ref[2:3, c0:c1] * cur)


def _block_kernel(tiles_per_seq, final_norm,
                  sinks_ref, pos_ref, x_ref, invf_ref, prm_ref, wcols_ref, wrows_ref, o_ref,
                  zbuf, kbuf, vbuf, qbuf, abuf, x1buf, u1buf, ubuf, hbuf):
    t = pl.program_id(0)
    tm = x_ref.shape[1]
    mixer_seq_start = lax.rem(t, tiles_per_seq) == 0
    ffn_seq_start = lax.rem(t + tiles_per_seq - 1, tiles_per_seq) == 0

    bin_ref = prm_ref.at[ROW_B_IN:ROW_B_IN + 1, :]
    nmw_ref = prm_ref.at[ROW_NORM_MIX:ROW_NORM_MIX + 1, 0:D_MODEL]
    bao_ref = prm_ref.at[ROW_B_ATTN_OUT:ROW_B_ATTN_OUT + 1, 0:D_MODEL]
    nfw_ref = prm_ref.at[ROW_NORM_FFN:ROW_NORM_FFN + 1, 0:D_MODEL]
    fin_ref = prm_ref.at[ROW_NORM_FINAL:ROW_NORM_FINAL + 1, 0:D_MODEL]
    cmw_ref = prm_ref.at[ROW_CONV_MIX:ROW_CONV_MIX + 3, 0:D_MODEL]
    fcw_ref = prm_ref.at[ROW_FFN_CONV:ROW_FFN_CONV + 3, 0:D_FF]
    fcb_ref = prm_ref.at[ROW_FFN_CONV_B:ROW_FFN_CONV_B + 1, 0:D_FF]
    win_ref = wcols_ref.at[:, COL_W_IN:COL_W_UP]
    wup_ref = wcols_ref.at[:, COL_W_UP:COL_W_CONV_OUT]
    wco_ref = wcols_ref.at[:, COL_W_CONV_OUT:COL_W_MIX_OUT]
    wmo_ref = wcols_ref.at[:, COL_W_MIX_OUT:COL_W_END]
    wdn_ref = wrows_ref.at[0:D_FF, :]
    wao_ref = wrows_ref.at[D_FF:D_FF + Q_WIDTH, :]

    @pl.when(t == 0)
    def _():
        x1buf[...] = jnp.zeros(x1buf.shape, F32)
        u1buf[...] = jnp.zeros(u1buf.shape, BF16)

    @pl.when(mixer_seq_start)
    def _():
        zbuf[0:HALO, :] = jnp.zeros((HALO, D_MODEL), F32)
        kbuf[:, 0:BLOCK, :] = jnp.zeros((4, BLOCK, LANES), BF16)
        vbuf[:, 0:BLOCK] = jnp.zeros((KV_WIDTH, BLOCK), BF16)

    @pl.when(jnp.logical_or(ffn_seq_start, t == 0))
    def _():
        ubuf[0:HALO, :] = jnp.zeros((HALO, D_FF), F32)

    x_prev = x1buf[...]
    u_prev = u1buf[...]
    x = x_ref[0]
    u = _rmsnorm(x, nmw_ref[...]).astype(BF16)

    def proj(off, width):
        return _dot(u, win_ref[:, off:off + width]) + bin_ref[:, off:off + width]

    def ffn_up(c0, c1):
        up = _dot(u_prev, wup_ref[:, c0:c1])
        ubuf[HALO:HALO + tm, c0:c1] = up
        a = _causal_conv3(ubuf, up, fcw_ref, tm, c0, c1) + fcb_ref[:, c0:c1]
        gate = _dot(u_prev, wup_ref[:, D_FF + c0:D_FF + c1])
        hbuf[:, c0:c1] = (jax.nn.silu(a) * gate).astype(BF16)

    ffn_up(*FF_CHUNKS[0])
    qkv = proj(OFF_Q, Q_WIDTH + 2 * KV_WIDTH)
    ffn_up(*FF_CHUNKS[1])

    ang = invf_ref[...] * pos_ref[0].astype(F32)
    cos_t, sin_t = jnp.cos(ang), jnp.sin(ang)
    cos = jnp.concatenate([cos_t] * 4, axis=0).T
    sin = jnp.concatenate([-sin_t, sin_t, -sin_t, sin_t], axis=0).T

    lane = lax.broadcasted_iota(jnp.int32, (tm, LANES), 1)
    first_half = (lane & (HEAD_DIM // 2)) == 0
    low_head = lane < HEAD_DIM

    def rope(v, c, s):
        partner = jnp.where(first_half, pltpu.roll(v, LANES - HEAD_DIM // 2, 1),
                            pltpu.roll(v, HEAD_DIM // 2, 1))
        return v * c + partner * s

    scale = HEAD_DIM ** -0.5
    cos_q, sin_q = cos * scale, sin * scale
    for c in range(Q_WIDTH // LANES):
        sl = slice(c * LANES, (c + 1) * LANES)
        qbuf[:, sl] = rope(qkv[:, sl], cos_q, sin_q).astype(BF16)

    k = rope(qkv[:, Q_WIDTH:Q_WIDTH + KV_WIDTH], cos, sin)
    krot = pltpu.roll(k, HEAD_DIM, 1)
    kzero = jnp.zeros_like(k)
    rows = slice(BLOCK, BLOCK + tm)
    kbuf[0, rows, :] = jnp.where(low_head, k, kzero).astype(BF16)
    kbuf[1, rows, :] = jnp.where(low_head, kzero, krot).astype(BF16)
    kbuf[2, rows, :] = jnp.where(low_head, krot, kzero).astype(BF16)
    kbuf[3, rows, :] = jnp.where(low_head, kzero, k).astype(BF16)
    vbuf[:, BLOCK:BLOCK + tm] = qkv[:, Q_WIDTH + KV_WIDTH:].T.astype(BF16)

    key2 = lax.broadcasted_iota(jnp.int32, (2 * BLOCK, 2 * BLOCK), 0)
    qry2 = lax.broadcasted_iota(jnp.int32, (2 * BLOCK, 2 * BLOCK), 1) & (BLOCK - 1)
    band = (key2 > qry2) & (key2 <= qry2 + BLOCK)
    neg_inf = jnp.full((2 * BLOCK, 2 * BLOCK), -jnp.inf, F32)
    zeros2 = jnp.zeros((2 * BLOCK, 2 * BLOCK), F32)
    bias = jnp.where(band, zeros2, neg_inf)
    bias_seq_start = jnp.where(band & (key2 >= BLOCK), zeros2, neg_inf)
    bias_first = jnp.where(mixer_seq_start, bias_seq_start, bias)

    z = proj(OFF_CC, D_MODEL) * proj(OFF_CX, D_MODEL)

    blocks = [(i, g) for i in range(tm // BLOCK) for g in range(N_KV_HEADS)]
    scores = {}
    for i, g in blocks:
        qrows = slice(i * BLOCK, (i + 1) * BLOCK)
        krows = slice(i * BLOCK, (i + 2) * BLOCK)
        blk_bias = bias_first if i == 0 else bias
        c0 = g * 2 * LANES
        qpair = jnp.concatenate([qbuf[qrows, c0:c0 + LANES],
                                 qbuf[qrows, c0 + LANES:c0 + 2 * LANES]], axis=0)
        scores[i, g] = (_dot_nt(kbuf[2 * g, krows, :], qpair) + blk_bias,
                        _dot_nt(kbuf[2 * g + 1, krows, :], qpair) + blk_bias)

    ffn_up(*FF_CHUNKS[2])
    ubuf[HALO - 2:HALO, :] = ubuf[HALO - 2 + tm:HALO + tm, :]

    top_pair = lax.broadcasted_iota(jnp.int32, (1, 2 * BLOCK), 1) < BLOCK
    sum_row = lax.broadcasted_iota(jnp.int32, (2 * HALO, 4 * BLOCK), 0)
    sum_col = lax.broadcasted_iota(jnp.int32, (2 * HALO, 4 * BLOCK), 1)
    sum_rows = jnp.where(sum_row == (sum_col >= 2 * BLOCK).astype(jnp.int32), 1.0, 0.0)
    sum_rows = sum_rows.astype(BF16)
    vzero = jnp.zeros((HEAD_DIM, 2 * BLOCK), BF16)

    for i, g in blocks:
        qrows = slice(i * BLOCK, (i + 1) * BLOCK)
        kcols = slice(i * BLOCK, (i + 2) * BLOCK)
        c0 = g * 2 * LANES
        s_even, s_odd = scores[i, g]
        sink_even = jnp.where(top_pair, sinks_ref[4 * g], sinks_ref[4 * g + 2])
        sink_odd = jnp.where(top_pair, sinks_ref[4 * g + 1], sinks_ref[4 * g + 3])
        m_even = jnp.maximum(jnp.max(s_even, axis=0, keepdims=True), sink_even)
        m_odd = jnp.maximum(jnp.max(s_odd, axis=0, keepdims=True), sink_odd)
        p_t = jnp.concatenate([jnp.exp(s_even - m_even).astype(BF16),
                               jnp.exp(s_odd - m_odd).astype(BF16)], axis=0)
        v_t = vbuf[g * HEAD_DIM:(g + 1) * HEAD_DIM, kcols]
        lhs = jnp.concatenate([jnp.concatenate([v_t, vzero], axis=1),
                               jnp.concatenate([vzero, v_t], axis=1), sum_rows], axis=0)
        r = _dot(lhs, p_t)
        inv_even = 1.0 / (r[2 * HEAD_DIM:2 * HEAD_DIM + 1] + jnp.exp(sink_even - m_even))
        inv_odd = 1.0 / (r[2 * HEAD_DIM + 1:2 * HEAD_DIM + 2] + jnp.exp(sink_odd - m_odd))
        out = jnp.concatenate([r[0:HEAD_DIM] * inv_even,
                               r[HEAD_DIM:2 * HEAD_DIM] * inv_odd], axis=0).T.astype(BF16)
        abuf[qrows, c0:c0 + LANES] = out[:BLOCK]
        abuf[qrows, c0 + LANES:c0 + 2 * LANES] = out[BLOCK:]

    kbuf[:, 0:BLOCK, :] = kbuf[:, tm:tm + BLOCK, :]
    vbuf[:, 0:BLOCK] = vbuf[:, tm:tm + BLOCK]

    zbuf[HALO:HALO + tm, :] = z
    conv = _causal_conv3(zbuf, z, cmw_ref, tm, 0, D_MODEL)
    zbuf[HALO - 2:HALO, :] = zbuf[HALO - 2 + tm:HALO + tm, :]
    gated = (proj(OFF_CB, D_MODEL) * conv).astype(BF16)
    gate_conv = proj(OFF_GC, D_MODEL)
    gate_attn = proj(OFF_GA, D_MODEL)
    y_conv = _dot(gated, wco_ref[...])

    y_attn = _dot(abuf[...], wao_ref[...]) + bao_ref[...]

    y = x_prev + _dot(hbuf[...], wdn_ref[...])
    o_ref[0] = _rmsnorm(y, fin_ref[...]) if final_norm else y

    merged = jax.nn.sigmoid(gate_conv) * y_conv + jax.nn.sigmoid(gate_attn) * y_attn
    x1 = x + _dot(merged.astype(BF16), wmo_ref[...])
    x1buf[...] = x1
    u1buf[...] = _rmsnorm(x1, nfw_ref[...]).astype(BF16)


def _whole(space=pltpu.VMEM):
    return pl.BlockSpec(memory_space=space)


def kernel(x, positions, norm_mix_w, w_in, b_in, conv_mix_w, w_conv_out, w_attn_out, b_attn_out,
           sinks, w_mix_out, norm_ffn_w, w_ffn_up, ffn_conv_w, ffn_conv_b, w_ffn_down, norm_final_w):
    b, s, d = x.shape
    depth = w_in.shape[0]
    tm = TOKEN_TILE
    assert d == D_MODEL and s % tm == 0 and tm % BLOCK == 0
    nj = s // tm
    n_tiles = b * nj

    def in_tile(t):
        return jnp.minimum(t, n_tiles - 1)

    x_spec = pl.BlockSpec((1, tm, D_MODEL), lambda t: (in_tile(t), 0, 0))
    pos_spec = pl.BlockSpec((1, 1, tm), lambda t: (in_tile(t), 0, 0))
    out_spec = pl.BlockSpec((1, tm, D_MODEL), lambda t: (jnp.maximum(t - 1, 0), 0, 0))

    inv_freq = (ROPE_THETA ** (-jnp.arange(0, HEAD_DIM, 2, dtype=F32) / HEAD_DIM)).reshape(-1, 1)
    pos_tiles = positions.reshape(n_tiles, 1, tm)

    def rows(v):
        v = v.reshape(-1, v.shape[-1])
        return jnp.pad(v, ((0, 0), (0, IN_WIDTH - v.shape[-1])))

    def layer(final_norm):
        return pl.pallas_call(
            functools.partial(_block_kernel, nj, final_norm),
            grid=(n_tiles + 1,),
            in_specs=[_whole(pltpu.SMEM), pos_spec, x_spec] + [_whole()] * 4,
            out_specs=out_spec,
            out_shape=jax.ShapeDtypeStruct((n_tiles, tm, D_MODEL), F32),
            scratch_shapes=[
                pltpu.VMEM((HALO + tm, D_MODEL), F32),
                pltpu.VMEM((4, BLOCK + tm, LANES), BF16),
                pltpu.VMEM((KV_WIDTH, BLOCK + tm), BF16),
                pltpu.VMEM((tm, Q_WIDTH), BF16),
                pltpu.VMEM((tm, Q_WIDTH), BF16),
                pltpu.VMEM((tm, D_MODEL), F32),
                pltpu.VMEM((tm, D_MODEL), BF16),
                pltpu.VMEM((HALO + tm, D_FF), F32),
                pltpu.VMEM((tm, D_FF), BF16),
            ],
            compiler_params=pltpu.CompilerParams(dimension_semantics=("arbitrary",),
                                                 vmem_limit_bytes=VMEM_LIMIT_BYTES),
            name="decoder_block",
        )

    x = x.reshape(n_tiles, tm, D_MODEL)
    for l in range(depth):
        params = jnp.concatenate(
            [rows(b_in[l]), rows(norm_mix_w[l]), rows(b_attn_out[l]), rows(norm_ffn_w[l]),
             rows(norm_final_w), rows(conv_mix_w[l]), rows(ffn_conv_w[l]), rows(ffn_conv_b[l])],
            axis=0)
        w_cols = jnp.concatenate([w_in[l], w_ffn_up[l], w_conv_out[l], w_mix_out[l]],
                                 axis=1).astype(BF16)
        w_rows = jnp.concatenate([w_ffn_down[l], w_attn_out[l]], axis=0).astype(BF16)
        x = layer(l == depth - 1)(sinks[l], pos_tiles, x, inv_freq, params, w_cols, w_rows)
    return x.reshape(b, s, D_MODEL)
```

```python
import functools

import jax
import jax.numpy as jnp
from jax import lax
from jax.experimental import pallas as pl
from jax.experimental.pallas import tpu as pltpu

D_MODEL = 1024
HEAD_DIM = 64
N_Q_HEADS = 8
N_KV_HEADS = 2
BLOCK = 128
ROPE_THETA = 10000.0
Q_WIDTH = N_Q_HEADS * HEAD_DIM
KV_WIDTH = N_KV_HEADS * HEAD_DIM
D_FF = 2816
EPS = 1e-5

OFF_CB = 0
OFF_CC = OFF_CB + D_MODEL
OFF_CX = OFF_CC + D_MODEL
OFF_Q = OFF_CX + D_MODEL
OFF_K = OFF_Q + Q_WIDTH
OFF_V = OFF_K + KV_WIDTH
OFF_GC = OFF_V + KV_WIDTH
OFF_GA = OFF_GC + D_MODEL
IN_WIDTH = OFF_GA + D_MODEL

ROW_B_IN, ROW_NORM_MIX, ROW_B_ATTN_OUT, ROW_NORM_FFN, ROW_NORM_FINAL = 0, 1, 2, 3, 4
ROW_CONV_MIX, ROW_FFN_CONV, ROW_FFN_CONV_B = 5, 8, 11

LANES = 128
HALO = 8
TOKEN_TILE = 256
FF_CHUNKS = ((0, 1024), (1024, 2048), (2048, D_FF))
VMEM_LIMIT_BYTES = 60 * 1024 * 1024

F32 = jnp.float32
BF16 = jnp.bfloat16


def _rmsnorm(x, w):
    ms = jnp.mean(x * x, axis=-1, keepdims=True)
    return x * lax.rsqrt(ms + EPS) * w


def _dot(a, b):
    return jnp.dot(a, b, preferred_element_type=F32)


def _dot_nt(a, b):
    return lax.dot_general(a, b, (((1,), (1,)), ((), ())), preferred_element_type=F32)


def _causal_conv3(buf, cur, taps_ref, tm, c0, c1):
    return (taps_ref[0:1, c0:c1] * buf[HALO - 2:HALO - 2 + tm, c0:c1]
            + taps_ref[1:2, c0:c1] * buf[HALO - 1:HALO - 1 + tm, c0:c1]
            + taps_ref[2:3, c0:c1] * cur)


def _block_kernel(tiles_per_seq, final_norm,
                  sinks_ref, pos_ref, x_ref, invf_ref, prm_ref, win_ref, wup_ref, wco_ref,
                  wmo_ref, wao_ref, wdn_ref, o_ref,
                  zbuf, kbuf, vbuf, qbuf, abuf, x1buf, u1buf, ubuf, hbuf):
    t = pl.program_id(0)
    tm = x_ref.shape[1]
    mixer_seq_start = lax.rem(t, tiles_per_seq) == 0
    ffn_seq_start = lax.rem(t + tiles_per_seq - 1, tiles_per_seq) == 0

    bin_ref = prm_ref.at[ROW_B_IN:ROW_B_IN + 1, :]
    nmw_ref = prm_ref.at[ROW_NORM_MIX:ROW_NORM_MIX + 1, 0:D_MODEL]
    bao_ref = prm_ref.at[ROW_B_ATTN_OUT:ROW_B_ATTN_OUT + 1, 0:D_MODEL]
    nfw_ref = prm_ref.at[ROW_NORM_FFN:ROW_NORM_FFN + 1, 0:D_MODEL]
    fin_ref = prm_ref.at[ROW_NORM_FINAL:ROW_NORM_FINAL + 1, 0:D_MODEL]
    cmw_ref = prm_ref.at[ROW_CONV_MIX:ROW_CONV_MIX + 3, 0:D_MODEL]
    fcw_ref = prm_ref.at[ROW_FFN_CONV:ROW_FFN_CONV + 3, 0:D_FF]
    fcb_ref = prm_ref.at[ROW_FFN_CONV_B:ROW_FFN_CONV_B + 1, 0:D_FF]

    @pl.when(t == 0)
    def _():
        x1buf[...] = jnp.zeros(x1buf.shape, F32)
        u1buf[...] = jnp.zeros(u1buf.shape, BF16)

    @pl.when(mixer_seq_start)
    def _():
        zbuf[0:HALO, :] = jnp.zeros((HALO, D_MODEL), F32)
        kbuf[:, 0:BLOCK, :] = jnp.zeros((4, BLOCK, LANES), BF16)
        vbuf[:, 0:BLOCK] = jnp.zeros((KV_WIDTH, BLOCK), BF16)

    @pl.when(jnp.logical_or(ffn_seq_start, t == 0))
    def _():
        ubuf[0:HALO, :] = jnp.zeros((HALO, D_FF), F32)

    x_prev = x1buf[...]
    u_prev = u1buf[...]
    x = x_ref[0]
    u = _rmsnorm(x, nmw_ref[...]).astype(BF16)

    def proj(off, width):
        return _dot(u, win_ref[:, off:off + width]) + bin_ref[:, off:off + width]

    def ffn_up(c0, c1):
        up = _dot(u_prev, wup_ref[:, c0:c1])
        ubuf[HALO:HALO + tm, c0:c1] = up
        a = _causal_conv3(ubuf, up, fcw_ref, tm, c0, c1) + fcb_ref[:, c0:c1]
        gate = _dot(u_prev, wup_ref[:, D_FF + c0:D_FF + c1])
        hbuf[:, c0:c1] = (jax.nn.silu(a) * gate).astype(BF16)

    ffn_up(*FF_CHUNKS[0])
    qkv = proj(OFF_Q, Q_WIDTH + 2 * KV_WIDTH)
    ffn_up(*FF_CHUNKS[1])

    ang = invf_ref[...] * pos_ref[0].astype(F32)
    cos_t, sin_t = jnp.cos(ang), jnp.sin(ang)
    cos = jnp.concatenate([cos_t] * 4, axis=0).T
    sin = jnp.concatenate([-sin_t, sin_t, -sin_t, sin_t], axis=0).T

    lane = lax.broadcasted_iota(jnp.int32, (tm, LANES), 1)
    first_half = (lane & (HEAD_DIM // 2)) == 0
    low_head = lane < HEAD_DIM

    def rope(v, c, s):
        partner = jnp.where(first_half, pltpu.roll(v, LANES - HEAD_DIM // 2, 1),
                            pltpu.roll(v, HEAD_DIM // 2, 1))
        return v * c + partner * s

    scale = HEAD_DIM ** -0.5
    cos_q, sin_q = cos * scale, sin * scale
    for c in range(Q_WIDTH // LANES):
        sl = slice(c * LANES, (c + 1) * LANES)
        qbuf[:, sl] = rope(qkv[:, sl], cos_q, sin_q).astype(BF16)

    k = rope(qkv[:, Q_WIDTH:Q_WIDTH + KV_WIDTH], cos, sin)
    krot = pltpu.roll(k, HEAD_DIM, 1)
    kzero = jnp.zeros_like(k)
    rows = slice(BLOCK, BLOCK + tm)
    kbuf[0, rows, :] = jnp.where(low_head, k, kzero).astype(BF16)
    kbuf[1, rows, :] = jnp.where(low_head, kzero, krot).astype(BF16)
    kbuf[2, rows, :] = jnp.where(low_head, krot, kzero).astype(BF16)
    kbuf[3, rows, :] = jnp.where(low_head, kzero, k).astype(BF16)
    vbuf[:, BLOCK:BLOCK + tm] = qkv[:, Q_WIDTH + KV_WIDTH:].T.astype(BF16)

    key2 = lax.broadcasted_iota(jnp.int32, (2 * BLOCK, 2 * BLOCK), 0)
    qry2 = lax.broadcasted_iota(jnp.int32, (2 * BLOCK, 2 * BLOCK), 1) & (BLOCK - 1)
    band = (key2 > qry2) & (key2 <= qry2 + BLOCK)
    neg_inf = jnp.full((2 * BLOCK, 2 * BLOCK), -jnp.inf, F32)
    zeros2 = jnp.zeros((2 * BLOCK, 2 * BLOCK), F32)
    bias = jnp.where(band, zeros2, neg_inf)
    bias_seq_start = jnp.where(band & (key2 >= BLOCK), zeros2, neg_inf)
    bias_first = jnp.where(mixer_seq_start, bias_seq_start, bias)

    z = proj(OFF_CC, D_MODEL) * proj(OFF_CX, D_MODEL)

    blocks = [(i, g) for i in range(tm // BLOCK) for g in range(N_KV_HEADS)]
    scores = {}
    for i, g in blocks:
        qrows = slice(i * BLOCK, (i + 1) * BLOCK)
        krows = slice(i * BLOCK, (i + 2) * BLOCK)
        blk_bias = bias_first if i == 0 else bias
        c0 = g * 2 * LANES
        qpair = jnp.concatenate([qbuf[qrows, c0:c0 + LANES],
                                 qbuf[qrows, c0 + LANES:c0 + 2 * LANES]], axis=0)
        scores[i, g] = (_dot_nt(kbuf[2 * g, krows, :], qpair) + blk_bias,
                        _dot_nt(kbuf[2 * g + 1, krows, :], qpair) + blk_bias)

    ffn_up(*FF_CHUNKS[2])
    ubuf[HALO - 2:HALO, :] = ubuf[HALO - 2 + tm:HALO + tm, :]

    top_pair = lax.broadcasted_iota(jnp.int32, (1, 2 * BLOCK), 1) < BLOCK
    sum_row = lax.broadcasted_iota(jnp.int32, (2 * HALO, 4 * BLOCK), 0)
    sum_col = lax.broadcasted_iota(jnp.int32, (2 * HALO, 4 * BLOCK), 1)
    sum_rows = jnp.where(sum_row == (sum_col >= 2 * BLOCK).astype(jnp.int32), 1.0, 0.0)
    sum_rows = sum_rows.astype(BF16)
    vzero = jnp.zeros((HEAD_DIM, 2 * BLOCK), BF16)

    for i, g in blocks:
        qrows = slice(i * BLOCK, (i + 1) * BLOCK)
        kcols = slice(i * BLOCK, (i + 2) * BLOCK)
        c0 = g * 2 * LANES
        s_even, s_odd = scores[i, g]
        sink_even = jnp.where(top_pair, sinks_ref[4 * g], sinks_ref[4 * g + 2])
        sink_odd = jnp.where(top_pair, sinks_ref[4 * g + 1], sinks_ref[4 * g + 3])
        m_even = jnp.maximum(jnp.max(s_even, axis=0, keepdims=True), sink_even)
        m_odd = jnp.maximum(jnp.max(s_odd, axis=0, keepdims=True), sink_odd)
        p_t = jnp.concatenate([jnp.exp(s_even - m_even).astype(BF16),
                               jnp.exp(s_odd - m_odd).astype(BF16)], axis=0)
        v_t = vbuf[g * HEAD_DIM:(g + 1) * HEAD_DIM, kcols]
        lhs = jnp.concatenate([jnp.concatenate([v_t, vzero], axis=1),
                               jnp.concatenate([vzero, v_t], axis=1), sum_rows], axis=0)
        r = _dot(lhs, p_t)
        inv_even = 1.0 / (r[2 * HEAD_DIM:2 * HEAD_DIM + 1] + jnp.exp(sink_even - m_even))
        inv_odd = 1.0 / (r[2 * HEAD_DIM + 1:2 * HEAD_DIM + 2] + jnp.exp(sink_odd - m_odd))
        out = jnp.concatenate([r[0:HEAD_DIM] * inv_even,
                               r[HEAD_DIM:2 * HEAD_DIM] * inv_odd], axis=0).T.astype(BF16)
        abuf[qrows, c0:c0 + LANES] = out[:BLOCK]
        abuf[qrows, c0 + LANES:c0 + 2 * LANES] = out[BLOCK:]

    kbuf[:, 0:BLOCK, :] = kbuf[:, tm:tm + BLOCK, :]
    vbuf[:, 0:BLOCK] = vbuf[:, tm:tm + BLOCK]

    zbuf[HALO:HALO + tm, :] = z
    conv = _causal_conv3(zbuf, z, cmw_ref, tm, 0, D_MODEL)
    zbuf[HALO - 2:HALO, :] = zbuf[HALO - 2 + tm:HALO + tm, :]
    gated = (proj(OFF_CB, D_MODEL) * conv).astype(BF16)
    gate_conv = proj(OFF_GC, D_MODEL)
    gate_attn = proj(OFF_GA, D_MODEL)
    y_conv = _dot(gated, wco_ref[...])

    y_attn = _dot(abuf[...], wao_ref[...]) + bao_ref[...]

    y = x_prev + _dot(hbuf[...], wdn_ref[...])
    o_ref[0] = _rmsnorm(y, fin_ref[...]) if final_norm else y

    merged = jax.nn.sigmoid(gate_conv) * y_conv + jax.nn.sigmoid(gate_attn) * y_attn
    x1 = x + _dot(merged.astype(BF16), wmo_ref[...])
    x1buf[...] = x1
    u1buf[...] = _rmsnorm(x1, nfw_ref[...]).astype(BF16)


def _whole(space=pltpu.VMEM):
    return pl.BlockSpec(memory_space=space)


def kernel(x, positions, norm_mix_w, w_in, b_in, conv_mix_w, w_conv_out, w_attn_out, b_attn_out,
           sinks, w_mix_out, norm_ffn_w, w_ffn_up, ffn_conv_w, ffn_conv_b, w_ffn_down, norm_final_w):
    b, s, d = x.shape
    depth = w_in.shape[0]
    tm = TOKEN_TILE
    assert d == D_MODEL and s % tm == 0 and tm % BLOCK == 0
    nj = s // tm
    n_tiles = b * nj

    def in_tile(t):
        return jnp.minimum(t, n_tiles - 1)

    x_spec = pl.BlockSpec((1, tm, D_MODEL), lambda t: (in_tile(t), 0, 0))
    pos_spec = pl.BlockSpec((1, 1, tm), lambda t: (in_tile(t), 0, 0))
    out_spec = pl.BlockSpec((1, tm, D_MODEL), lambda t: (jnp.maximum(t - 1, 0), 0, 0))

    inv_freq = (ROPE_THETA ** (-jnp.arange(0, HEAD_DIM, 2, dtype=F32) / HEAD_DIM)).reshape(-1, 1)
    pos_tiles = positions.reshape(n_tiles, 1, tm)

    def rows(v):
        v = v.reshape(-1, v.shape[-1])
        return jnp.pad(v, ((0, 0), (0, IN_WIDTH - v.shape[-1])))

    def layer(final_norm):
        return pl.pallas_call(
            functools.partial(_block_kernel, nj, final_norm),
            grid=(n_tiles + 1,),
            in_specs=[_whole(pltpu.SMEM), pos_spec, x_spec] + [_whole()] * 8,
            out_specs=out_spec,
            out_shape=jax.ShapeDtypeStruct((n_tiles, tm, D_MODEL), F32),
            scratch_shapes=[
                pltpu.VMEM((HALO + tm, D_MODEL), F32),
                pltpu.VMEM((4, BLOCK + tm, LANES), BF16),
                pltpu.VMEM((KV_WIDTH, BLOCK + tm), BF16),
                pltpu.VMEM((tm, Q_WIDTH), BF16),
                pltpu.VMEM((tm, Q_WIDTH), BF16),
                pltpu.VMEM((tm, D_MODEL), F32),
                pltpu.VMEM((tm, D_MODEL), BF16),
                pltpu.VMEM((HALO + tm, D_FF), F32),
                pltpu.VMEM((tm, D_FF), BF16),
            ],
            compiler_params=pltpu.CompilerParams(dimension_semantics=("arbitrary",),
                                                 vmem_limit_bytes=VMEM_LIMIT_BYTES),
            name="decoder_block",
        )

    x = x.reshape(n_tiles, tm, D_MODEL)
    for l in range(depth):
        params = jnp.concatenate(
            [rows(b_in[l]), rows(norm_mix_w[l]), rows(b_attn_out[l]), rows(norm_ffn_w[l]),
             rows(norm_final_w), rows(conv_mix_w[l]), rows(ffn_conv_w[l]), rows(ffn_conv_b[l])],
            axis=0)
        x = layer(l == depth - 1)(
            sinks[l], pos_tiles, x, inv_freq, params, w_in[l].astype(BF16),
            w_ffn_up[l].astype(BF16), w_conv_out[l].astype(BF16), w_mix_out[l].astype(BF16),
            w_attn_out[l].astype(BF16), w_ffn_down[l].astype(BF16))
    return x.reshape(b, s, D_MODEL)
```

```python
import functools

import jax
import jax.numpy as jnp
from jax import lax
from jax.experimental import pallas as pl
from jax.experimental.pallas import tpu as pltpu

D_MODEL = 1024
HEAD_DIM = 64
N_Q_HEADS = 8
N_KV_HEADS = 2
BLOCK = 128
ROPE_THETA = 10000.0
Q_WIDTH = N_Q_HEADS * HEAD_DIM
KV_WIDTH = N_KV_HEADS * HEAD_DIM
D_FF = 2816
EPS = 1e-5

OFF_CB = 0
OFF_CC = OFF_CB + D_MODEL
OFF_CX = OFF_CC + D_MODEL
OFF_Q = OFF_CX + D_MODEL
OFF_K = OFF_Q + Q_WIDTH
OFF_V = OFF_K + KV_WIDTH
OFF_GC = OFF_V + KV_WIDTH
OFF_GA = OFF_GC + D_MODEL
IN_WIDTH = OFF_GA + D_MODEL

ROW_B_IN, ROW_NORM_MIX, ROW_B_ATTN_OUT, ROW_NORM_FFN, ROW_NORM_FINAL = 0, 1, 2, 3, 4
ROW_CONV_MIX, ROW_FFN_CONV, ROW_FFN_CONV_B = 5, 8, 11

LANES = 128
HALO = 8
TOKEN_TILE = 256
FF_CHUNK = 512
VMEM_LIMIT_BYTES = 60 * 1024 * 1024

F32 = jnp.float32
BF16 = jnp.bfloat16


def _rmsnorm(x, w):
    ms = jnp.mean(x * x, axis=-1, keepdims=True)
    return x * lax.rsqrt(ms + EPS) * w


def _dot(a, b):
    return jnp.dot(a, b, preferred_element_type=F32)


def _dot_nt(a, b):
    return lax.dot_general(a, b, (((1,), (1,)), ((), ())), preferred_element_type=F32)


def _causal_conv3(buf, cur, taps_ref, tm, c0, c1):
    return (taps_ref[0:1, c0:c1] * buf[HALO - 2:HALO - 2 + tm, c0:c1]
            + taps_ref[1:2, c0:c1] * buf[HALO - 1:HALO - 1 + tm, c0:c1]
            + taps_ref[2:3, c0:c1] * cur)


def _block_kernel(tiles_per_seq, final_norm,
                  sinks_ref, pos_ref, x_ref, invf_ref, prm_ref, win_ref, wup_ref, wco_ref,
                  wmo_ref, wao_ref, wdn_ref, o_ref,
                  zbuf, kbuf, vbuf, qbuf, abuf, x1buf, u1buf, ubuf, hbuf):
    t = pl.program_id(0)
    tm = x_ref.shape[1]
    mixer_seq_start = lax.rem(t, tiles_per_seq) == 0
    ffn_seq_start = lax.rem(t + tiles_per_seq - 1, tiles_per_seq) == 0

    bin_ref = prm_ref.at[ROW_B_IN:ROW_B_IN + 1, :]
    nmw_ref = prm_ref.at[ROW_NORM_MIX:ROW_NORM_MIX + 1, 0:D_MODEL]
    bao_ref = prm_ref.at[ROW_B_ATTN_OUT:ROW_B_ATTN_OUT + 1, 0:D_MODEL]
    nfw_ref = prm_ref.at[ROW_NORM_FFN:ROW_NORM_FFN + 1, 0:D_MODEL]
    fin_ref = prm_ref.at[ROW_NORM_FINAL:ROW_NORM_FINAL + 1, 0:D_MODEL]
    cmw_ref = prm_ref.at[ROW_CONV_MIX:ROW_CONV_MIX + 3, 0:D_MODEL]
    fcw_ref = prm_ref.at[ROW_FFN_CONV:ROW_FFN_CONV + 3, 0:D_FF]
    fcb_ref = prm_ref.at[ROW_FFN_CONV_B:ROW_FFN_CONV_B + 1, 0:D_FF]

    @pl.when(t == 0)
    def _():
        x1buf[...] = jnp.zeros(x1buf.shape, F32)
        u1buf[...] = jnp.zeros(u1buf.shape, BF16)

    @pl.when(mixer_seq_start)
    def _():
        zbuf[0:HALO, :] = jnp.zeros((HALO, D_MODEL), F32)
        kbuf[:, 0:BLOCK, :] = jnp.zeros((4, BLOCK, LANES), BF16)
        vbuf[:, 0:BLOCK] = jnp.zeros((KV_WIDTH, BLOCK), BF16)

    @pl.when(jnp.logical_or(ffn_seq_start, t == 0))
    def _():
        ubuf[0:HALO, :] = jnp.zeros((HALO, D_FF), F32)

    x_prev = x1buf[...]
    u_prev = u1buf[...]
    x = x_ref[0]
    u = _rmsnorm(x, nmw_ref[...]).astype(BF16)

    def proj(off, width):
        return _dot(u, win_ref[:, off:off + width]) + bin_ref[:, off:off + width]

    def ffn_up(c0, c1):
        up = _dot(u_prev, wup_ref[:, c0:c1])
        ubuf[HALO:HALO + tm, c0:c1] = up
        a = _causal_conv3(ubuf, up, fcw_ref, tm, c0, c1) + fcb_ref[:, c0:c1]
        gate = _dot(u_prev, wup_ref[:, D_FF + c0:D_FF + c1])
        hbuf[:, c0:c1] = (jax.nn.silu(a) * gate).astype(BF16)

    chunk_starts = iter(range(0, D_FF, FF_CHUNK))

    def ffn_chunks(n):
        for _ in range(n):
            c0 = next(chunk_starts)
            ffn_up(c0, min(c0 + FF_CHUNK, D_FF))

    ffn_chunks(1)
    qkv = proj(OFF_Q, Q_WIDTH + 2 * KV_WIDTH)
    ffn_chunks(1)

    ang = invf_ref[...] * pos_ref[0].astype(F32)
    cos_t, sin_t = jnp.cos(ang), jnp.sin(ang)
    cos = jnp.concatenate([cos_t] * 4, axis=0).T
    sin = jnp.concatenate([-sin_t, sin_t, -sin_t, sin_t], axis=0).T

    lane = lax.broadcasted_iota(jnp.int32, (tm, LANES), 1)
    first_half = (lane & (HEAD_DIM // 2)) == 0
    low_head = lane < HEAD_DIM

    def rope(v, c, s):
        partner = jnp.where(first_half, pltpu.roll(v, LANES - HEAD_DIM // 2, 1),
                            pltpu.roll(v, HEAD_DIM // 2, 1))
        return v * c + partner * s

    scale = HEAD_DIM ** -0.5
    cos_q, sin_q = cos * scale, sin * scale
    for c in range(Q_WIDTH // LANES):
        sl = slice(c * LANES, (c + 1) * LANES)
        qbuf[:, sl] = rope(qkv[:, sl], cos_q, sin_q).astype(BF16)

    k = rope(qkv[:, Q_WIDTH:Q_WIDTH + KV_WIDTH], cos, sin)
    krot = pltpu.roll(k, HEAD_DIM, 1)
    kzero = jnp.zeros_like(k)
    rows = slice(BLOCK, BLOCK + tm)
    kbuf[0, rows, :] = jnp.where(low_head, k, kzero).astype(BF16)
    kbuf[1, rows, :] = jnp.where(low_head, kzero, krot).astype(BF16)
    kbuf[2, rows, :] = jnp.where(low_head, krot, kzero).astype(BF16)
    kbuf[3, rows, :] = jnp.where(low_head, kzero, k).astype(BF16)
    vbuf[:, BLOCK:BLOCK + tm] = qkv[:, Q_WIDTH + KV_WIDTH:].T.astype(BF16)

    key2 = lax.broadcasted_iota(jnp.int32, (2 * BLOCK, 2 * BLOCK), 0)
    qry2 = lax.broadcasted_iota(jnp.int32, (2 * BLOCK, 2 * BLOCK), 1) & (BLOCK - 1)
    band = (key2 > qry2) & (key2 <= qry2 + BLOCK)
    neg_inf = jnp.full((2 * BLOCK, 2 * BLOCK), -jnp.inf, F32)
    zeros2 = jnp.zeros((2 * BLOCK, 2 * BLOCK), F32)
    bias = jnp.where(band, zeros2, neg_inf)
    bias_seq_start = jnp.where(band & (key2 >= BLOCK), zeros2, neg_inf)
    bias_first = jnp.where(mixer_seq_start, bias_seq_start, bias)

    conv_c = proj(OFF_CC, D_MODEL)
    ffn_chunks(1)
    z = conv_c * proj(OFF_CX, D_MODEL)

    blocks = [(i, g) for i in range(tm // BLOCK) for g in range(N_KV_HEADS)]
    scores = {}
    for i, g in blocks:
        qrows = slice(i * BLOCK, (i + 1) * BLOCK)
        krows = slice(i * BLOCK, (i + 2) * BLOCK)
        blk_bias = bias_first if i == 0 else bias
        c0 = g * 2 * LANES
        qpair = jnp.concatenate([qbuf[qrows, c0:c0 + LANES],
                                 qbuf[qrows, c0 + LANES:c0 + 2 * LANES]], axis=0)
        scores[i, g] = (_dot_nt(kbuf[2 * g, krows, :], qpair) + blk_bias,
                        _dot_nt(kbuf[2 * g + 1, krows, :], qpair) + blk_bias)

    ffn_chunks(1)

    top_pair = lax.broadcasted_iota(jnp.int32, (1, 2 * BLOCK), 1) < BLOCK
    sum_row = lax.broadcasted_iota(jnp.int32, (2 * HALO, 4 * BLOCK), 0)
    sum_col = lax.broadcasted_iota(jnp.int32, (2 * HALO, 4 * BLOCK), 1)
    sum_rows = jnp.where(sum_row == (sum_col >= 2 * BLOCK).astype(jnp.int32), 1.0, 0.0)
    sum_rows = sum_rows.astype(BF16)
    vzero = jnp.zeros((HEAD_DIM, 2 * BLOCK), BF16)

    for i, g in blocks:
        qrows = slice(i * BLOCK, (i + 1) * BLOCK)
        kcols = slice(i * BLOCK, (i + 2) * BLOCK)
        c0 = g * 2 * LANES
        s_even, s_odd = scores[i, g]
        sink_even = jnp.where(top_pair, sinks_ref[4 * g], sinks_ref[4 * g + 2])
        sink_odd = jnp.where(top_pair, sinks_ref[4 * g + 1], sinks_ref[4 * g + 3])
        m_even = jnp.maximum(jnp.max(s_even, axis=0, keepdims=True), sink_even)
        m_odd = jnp.maximum(jnp.max(s_odd, axis=0, keepdims=True), sink_odd)
        p_t = jnp.concatenate([jnp.exp(s_even - m_even).astype(BF16),
                               jnp.exp(s_odd - m_odd).astype(BF16)], axis=0)
        v_t = vbuf[g * HEAD_DIM:(g + 1) * HEAD_DIM, kcols]
        lhs = jnp.concatenate([jnp.concatenate([v_t, vzero], axis=1),
                               jnp.concatenate([vzero, v_t], axis=1), sum_rows], axis=0)
        r = _dot(lhs, p_t)
        inv_even = 1.0 / (r[2 * HEAD_DIM:2 * HEAD_DIM + 1] + jnp.exp(sink_even - m_even))
        inv_odd = 1.0 / (r[2 * HEAD_DIM + 1:2 * HEAD_DIM + 2] + jnp.exp(sink_odd - m_odd))
        out = jnp.concatenate([r[0:HEAD_DIM] * inv_even,
                               r[HEAD_DIM:2 * HEAD_DIM] * inv_odd], axis=0).T.astype(BF16)
        abuf[qrows, c0:c0 + LANES] = out[:BLOCK]
        abuf[qrows, c0 + LANES:c0 + 2 * LANES] = out[BLOCK:]

    kbuf[:, 0:BLOCK, :] = kbuf[:, tm:tm + BLOCK, :]
    vbuf[:, 0:BLOCK] = vbuf[:, tm:tm + BLOCK]

    ffn_chunks(1)

    zbuf[HALO:HALO + tm, :] = z
    conv = _causal_conv3(zbuf, z, cmw_ref, tm, 0, D_MODEL)
    zbuf[HALO - 2:HALO, :] = zbuf[HALO - 2 + tm:HALO + tm, :]
    gated = (proj(OFF_CB, D_MODEL) * conv).astype(BF16)
    ffn_chunks(1)
    gate_conv = proj(OFF_GC, D_MODEL)
    gate_attn = proj(OFF_GA, D_MODEL)
    ubuf[HALO - 2:HALO, :] = ubuf[HALO - 2 + tm:HALO + tm, :]
    y_conv = _dot(gated, wco_ref[...])

    y_attn = _dot(abuf[...], wao_ref[...]) + bao_ref[...]

    y = x_prev + _dot(hbuf[...], wdn_ref[...])
    o_ref[0] = _rmsnorm(y, fin_ref[...]) if final_norm else y

    merged = jax.nn.sigmoid(gate_conv) * y_conv + jax.nn.sigmoid(gate_attn) * y_attn
    x1 = x + _dot(merged.astype(BF16), wmo_ref[...])
    x1buf[...] = x1
    u1buf[...] = _rmsnorm(x1, nfw_ref[...]).astype(BF16)


def _whole(space=pltpu.VMEM):
    return pl.BlockSpec(memory_space=space)


def kernel(x, positions, norm_mix_w, w_in, b_in, conv_mix_w, w_conv_out, w_attn_out, b_attn_out,
           sinks, w_mix_out, norm_ffn_w, w_ffn_up, ffn_conv_w, ffn_conv_b, w_ffn_down, norm_final_w):
    b, s, d = x.shape
    depth = w_in.shape[0]
    tm = TOKEN_TILE
    assert d == D_MODEL and s % tm == 0 and tm % BLOCK == 0
    nj = s // tm
    n_tiles = b * nj

    def in_tile(t):
        return jnp.minimum(t, n_tiles - 1)

    x_spec = pl.BlockSpec((1, tm, D_MODEL), lambda t: (in_tile(t), 0, 0))
    pos_spec = pl.BlockSpec((1, 1, tm), lambda t: (in_tile(t), 0, 0))
    out_spec = pl.BlockSpec((1, tm, D_MODEL), lambda t: (jnp.maximum(t - 1, 0), 0, 0))

    inv_freq = (ROPE_THETA ** (-jnp.arange(0, HEAD_DIM, 2, dtype=F32) / HEAD_DIM)).reshape(-1, 1)
    pos_tiles = positions.reshape(n_tiles, 1, tm)

    def rows(v):
        v = v.reshape(-1, v.shape[-1])
        return jnp.pad(v, ((0, 0), (0, IN_WIDTH - v.shape[-1])))

    def layer(final_norm):
        return pl.pallas_call(
            functools.partial(_block_kernel, nj, final_norm),
            grid=(n_tiles + 1,),
            in_specs=[_whole(pltpu.SMEM), pos_spec, x_spec] + [_whole()] * 8,
            out_specs=out_spec,
            out_shape=jax.ShapeDtypeStruct((n_tiles, tm, D_MODEL), F32),
            scratch_shapes=[
                pltpu.VMEM((HALO + tm, D_MODEL), F32),
                pltpu.VMEM((4, BLOCK + tm, LANES), BF16),
                pltpu.VMEM((KV_WIDTH, BLOCK + tm), BF16),
                pltpu.VMEM((tm, Q_WIDTH), BF16),
                pltpu.VMEM((tm, Q_WIDTH), BF16),
                pltpu.VMEM((tm, D_MODEL), F32),
                pltpu.VMEM((tm, D_MODEL), BF16),
                pltpu.VMEM((HALO + tm, D_FF), F32),
                pltpu.VMEM((tm, D_FF), BF16),
            ],
            compiler_params=pltpu.CompilerParams(dimension_semantics=("arbitrary",),
                                                 vmem_limit_bytes=VMEM_LIMIT_BYTES),
            name="decoder_block",
        )

    x = x.reshape(n_tiles, tm, D_MODEL)
    for l in range(depth):
        params = jnp.concatenate(
            [rows(b_in[l]), rows(norm_mix_w[l]), rows(b_attn_out[l]), rows(norm_ffn_w[l]),
             rows(norm_final_w), rows(conv_mix_w[l]), rows(ffn_conv_w[l]), rows(ffn_conv_b[l])],
            axis=0)
        x = layer(l == depth - 1)(
            sinks[l], pos_tiles, x, inv_freq, params, w_in[l].astype(BF16),
            w_ffn_up[l].astype(BF16), w_conv_out[l].astype(BF16), w_mix_out[l].astype(BF16),
            w_attn_out[l].astype(BF16), w_ffn_down[l].astype(BF16))
    return x.reshape(b, s, D_MODEL)
```

```python
import functools

import jax
import jax.numpy as jnp
from jax import lax
from jax.experimental import pallas as pl
from jax.experimental.pallas import tpu as pltpu

D_MODEL = 1024
HEAD_DIM = 64
N_Q_HEADS = 8
N_KV_HEADS = 2
BLOCK = 128
ROPE_THETA = 10000.0
Q_WIDTH = N_Q_HEADS * HEAD_DIM
KV_WIDTH = N_KV_HEADS * HEAD_DIM
D_FF = 2816
EPS = 1e-5

OFF_CB = 0
OFF_CC = OFF_CB + D_MODEL
OFF_CX = OFF_CC + D_MODEL
OFF_Q = OFF_CX + D_MODEL
OFF_K = OFF_Q + Q_WIDTH
OFF_V = OFF_K + KV_WIDTH
OFF_GC = OFF_V + KV_WIDTH
OFF_GA = OFF_GC + D_MODEL
IN_WIDTH = OFF_GA + D_MODEL

ROW_B_IN, ROW_NORM_MIX, ROW_B_ATTN_OUT, ROW_NORM_FFN, ROW_NORM_FINAL = 0, 1, 2, 3, 4
ROW_CONV_MIX, ROW_FFN_CONV, ROW_FFN_CONV_B = 5, 8, 11

LANES = 128
HALO = 8
TOKEN_TILE = 256
FF_CHUNKS = ((0, 1024), (1024, 2048), (2048, D_FF))
VMEM_LIMIT_BYTES = 60 * 1024 * 1024

STAGE_SLOTS = 4
STAGE_ROWS_WIDE = 32
STAGE_ROWS_NARROW = 256

F32 = jnp.float32
BF16 = jnp.bfloat16


def _rmsnorm(x, w):
    ms = jnp.mean(x * x, axis=-1, keepdims=True)
    return x * lax.rsqrt(ms + EPS) * w


def _dot(a, b):
    return jnp.dot(a, b, preferred_element_type=F32)


def _dot_nt(a, b):
    return lax.dot_general(a, b, (((1,), (1,)), ((), ())), preferred_element_type=F32)


def _causal_conv3(buf, cur, taps_ref, tm, c0, c1):
    return (taps_ref[0:1, c0:c1] * buf[HALO - 2:HALO - 2 + tm, c0:c1]
            + taps_ref[1:2, c0:c1] * buf[HALO - 1:HALO - 1 + tm, c0:c1]
            + taps_ref[2:3, c0:c1] * cur)


def _load_as_bf16(src_hbm, dst_ref, stage, sems):
    rows, width = src_hbm.shape
    chunk = stage.shape[1]
    assert rows % chunk == 0 and width <= stage.shape[2]
    n = rows // chunk
    ahead = STAGE_SLOTS - 1

    def copy(i, slot):
        r0 = i * chunk if isinstance(i, int) else pl.multiple_of(i * chunk, chunk)
        return pltpu.make_async_copy(src_hbm.at[pl.ds(r0, chunk), :],
                                     stage.at[slot, :, 0:width], sems.at[slot])

    for i in range(min(ahead, n)):
        copy(i, i).start()

    def body(i, carry):
        slot = lax.rem(i, STAGE_SLOTS)
        copy(i, slot).wait()

        @pl.when(i + ahead < n)
        def _():
            copy(i + ahead, lax.rem(i + ahead, STAGE_SLOTS)).start()

        r0 = pl.multiple_of(i * chunk, chunk)
        dst_ref[pl.ds(r0, chunk), :] = stage[slot, :, 0:width].astype(BF16)
        return carry

    lax.fori_loop(0, n, body, 0)


def _block_kernel(tiles_per_seq, final_norm,
                  sinks_ref, pos_ref, x_ref, invf_ref, prm_ref, win_hbm, wup_hbm, wco_hbm,
                  wmo_hbm, wao_hbm, wdn_hbm, o_ref,
                  zbuf, kbuf, vbuf, qbuf, abuf, x1buf, u1buf, ubuf, hbuf,
                  win_ref, wup_ref, wco_ref, wmo_ref, wao_ref, wdn_ref,
                  stage_wide, stage_narrow, stage_sems):
    t = pl.program_id(0)
    tm = x_ref.shape[1]
    mixer_seq_start = lax.rem(t, tiles_per_seq) == 0
    ffn_seq_start = lax.rem(t + tiles_per_seq - 1, tiles_per_seq) == 0

    bin_ref = prm_ref.at[ROW_B_IN:ROW_B_IN + 1, :]
    nmw_ref = prm_ref.at[ROW_NORM_MIX:ROW_NORM_MIX + 1, 0:D_MODEL]
    bao_ref = prm_ref.at[ROW_B_ATTN_OUT:ROW_B_ATTN_OUT + 1, 0:D_MODEL]
    nfw_ref = prm_ref.at[ROW_NORM_FFN:ROW_NORM_FFN + 1, 0:D_MODEL]
    fin_ref = prm_ref.at[ROW_NORM_FINAL:ROW_NORM_FINAL + 1, 0:D_MODEL]
    cmw_ref = prm_ref.at[ROW_CONV_MIX:ROW_CONV_MIX + 3, 0:D_MODEL]
    fcw_ref = prm_ref.at[ROW_FFN_CONV:ROW_FFN_CONV + 3, 0:D_FF]
    fcb_ref = prm_ref.at[ROW_FFN_CONV_B:ROW_FFN_CONV_B + 1, 0:D_FF]

    @pl.when(t == 0)
    def _():
        x1buf[...] = jnp.zeros(x1buf.shape, F32)
        u1buf[...] = jnp.zeros(u1buf.shape, BF16)
        _load_as_bf16(win_hbm, win_ref, stage_wide, stage_sems)
        _load_as_bf16(wup_hbm, wup_ref, stage_wide, stage_sems)
        _load_as_bf16(wco_hbm, wco_ref, stage_narrow, stage_sems)
        _load_as_bf16(wmo_hbm, wmo_ref, stage_narrow, stage_sems)
        _load_as_bf16(wao_hbm, wao_ref, stage_narrow, stage_sems)
        _load_as_bf16(wdn_hbm, wdn_ref, stage_narrow, stage_sems)

    @pl.when(mixer_seq_start)
    def _():
        zbuf[0:HALO, :] = jnp.zeros((HALO, D_MODEL), F32)
        kbuf[:, 0:BLOCK, :] = jnp.zeros((4, BLOCK, LANES), BF16)
        vbuf[:, 0:BLOCK] = jnp.zeros((KV_WIDTH, BLOCK), BF16)

    @pl.when(jnp.logical_or(ffn_seq_start, t == 0))
    def _():
        ubuf[0:HALO, :] = jnp.zeros((HALO, D_FF), F32)

    x_prev = x1buf[...]
    u_prev = u1buf[...]
    x = x_ref[0]
    u = _rmsnorm(x, nmw_ref[...]).astype(BF16)

    def proj(off, width):
        return _dot(u, win_ref[:, off:off + width]) + bin_ref[:, off:off + width]

    def ffn_up(c0, c1):
        up = _dot(u_prev, wup_ref[:, c0:c1])
        ubuf[HALO:HALO + tm, c0:c1] = up
        a = _causal_conv3(ubuf, up, fcw_ref, tm, c0, c1) + fcb_ref[:, c0:c1]
        gate = _dot(u_prev, wup_ref[:, D_FF + c0:D_FF + c1])
        hbuf[:, c0:c1] = (jax.nn.silu(a) * gate).astype(BF16)

    ffn_up(*FF_CHUNKS[0])
    qkv = proj(OFF_Q, Q_WIDTH + 2 * KV_WIDTH)
    ffn_up(*FF_CHUNKS[1])

    ang = invf_ref[...] * pos_ref[0].astype(F32)
    cos_t, sin_t = jnp.cos(ang), jnp.sin(ang)
    cos = jnp.concatenate([cos_t] * 4, axis=0).T
    sin = jnp.concatenate([-sin_t, sin_t, -sin_t, sin_t], axis=0).T

    lane = lax.broadcasted_iota(jnp.int32, (tm, LANES), 1)
    first_half = (lane & (HEAD_DIM // 2)) == 0
    low_head = lane < HEAD_DIM

    def rope(v, c, s):
        partner = jnp.where(first_half, pltpu.roll(v, LANES - HEAD_DIM // 2, 1),
                            pltpu.roll(v, HEAD_DIM // 2, 1))
        return v * c + partner * s

    scale = HEAD_DIM ** -0.5
    cos_q, sin_q = cos * scale, sin * scale
    for c in range(Q_WIDTH // LANES):
        sl = slice(c * LANES, (c + 1) * LANES)
        qbuf[:, sl] = rope(qkv[:, sl], cos_q, sin_q).astype(BF16)

    k = rope(qkv[:, Q_WIDTH:Q_WIDTH + KV_WIDTH], cos, sin)
    krot = pltpu.roll(k, HEAD_DIM, 1)
    kzero = jnp.zeros_like(k)
    rows = slice(BLOCK, BLOCK + tm)
    kbuf[0, rows, :] = jnp.where(low_head, k, kzero).astype(BF16)
    kbuf[1, rows, :] = jnp.where(low_head, kzero, krot).astype(BF16)
    kbuf[2, rows, :] = jnp.where(low_head, krot, kzero).astype(BF16)
    kbuf[3, rows, :] = jnp.where(low_head, kzero, k).astype(BF16)
    vbuf[:, BLOCK:BLOCK + tm] = qkv[:, Q_WIDTH + KV_WIDTH:].T.astype(BF16)

    key2 = lax.broadcasted_iota(jnp.int32, (2 * BLOCK, 2 * BLOCK), 0)
    qry2 = lax.broadcasted_iota(jnp.int32, (2 * BLOCK, 2 * BLOCK), 1) & (BLOCK - 1)
    band = (key2 > qry2) & (key2 <= qry2 + BLOCK)
    neg_inf = jnp.full((2 * BLOCK, 2 * BLOCK), -jnp.inf, F32)
    zeros2 = jnp.zeros((2 * BLOCK, 2 * BLOCK), F32)
    bias = jnp.where(band, zeros2, neg_inf)
    bias_seq_start = jnp.where(band & (key2 >= BLOCK), zeros2, neg_inf)
    bias_first = jnp.where(mixer_seq_start, bias_seq_start, bias)

    z = proj(OFF_CC, D_MODEL) * proj(OFF_CX, D_MODEL)

    blocks = [(i, g) for i in range(tm // BLOCK) for g in range(N_KV_HEADS)]
    scores = {}
    for i, g in blocks:
        qrows = slice(i * BLOCK, (i + 1) * BLOCK)
        krows = slice(i * BLOCK, (i + 2) * BLOCK)
        blk_bias = bias_first if i == 0 else bias
        c0 = g * 2 * LANES
        qpair = jnp.concatenate([qbuf[qrows, c0:c0 + LANES],
                                 qbuf[qrows, c0 + LANES:c0 + 2 * LANES]], axis=0)
        scores[i, g] = (_dot_nt(kbuf[2 * g, krows, :], qpair) + blk_bias,
                        _dot_nt(kbuf[2 * g + 1, krows, :], qpair) + blk_bias)

    ffn_up(*FF_CHUNKS[2])
    ubuf[HALO - 2:HALO, :] = ubuf[HALO - 2 + tm:HALO + tm, :]

    top_pair = lax.broadcasted_iota(jnp.int32, (1, 2 * BLOCK), 1) < BLOCK
    sum_row = lax.broadcasted_iota(jnp.int32, (2 * HALO, 4 * BLOCK), 0)
    sum_col = lax.broadcasted_iota(jnp.int32, (2 * HALO, 4 * BLOCK), 1)
    sum_rows = jnp.where(sum_row == (sum_col >= 2 * BLOCK).astype(jnp.int32), 1.0, 0.0)
    sum_rows = sum_rows.astype(BF16)
    vzero = jnp.zeros((HEAD_DIM, 2 * BLOCK), BF16)

    for i, g in blocks:
        qrows = slice(i * BLOCK, (i + 1) * BLOCK)
        kcols = slice(i * BLOCK, (i + 2) * BLOCK)
        c0 = g * 2 * LANES
        s_even, s_odd = scores[i, g]
        sink_even = jnp.where(top_pair, sinks_ref[4 * g], sinks_ref[4 * g + 2])
        sink_odd = jnp.where(top_pair, sinks_ref[4 * g + 1], sinks_ref[4 * g + 3])
        m_even = jnp.maximum(jnp.max(s_even, axis=0, keepdims=True), sink_even)
        m_odd = jnp.maximum(jnp.max(s_odd, axis=0, keepdims=True), sink_odd)
        p_t = jnp.concatenate([jnp.exp(s_even - m_even).astype(BF16),
                               jnp.exp(s_odd - m_odd).astype(BF16)], axis=0)
        v_t = vbuf[g * HEAD_DIM:(g + 1) * HEAD_DIM, kcols]
        lhs = jnp.concatenate([jnp.concatenate([v_t, vzero], axis=1),
                               jnp.concatenate([vzero, v_t], axis=1), sum_rows], axis=0)
        r = _dot(lhs, p_t)
        inv_even = 1.0 / (r[2 * HEAD_DIM:2 * HEAD_DIM + 1] + jnp.exp(sink_even - m_even))
        inv_odd = 1.0 / (r[2 * HEAD_DIM + 1:2 * HEAD_DIM + 2] + jnp.exp(sink_odd - m_odd))
        out = jnp.concatenate([r[0:HEAD_DIM] * inv_even,
                               r[HEAD_DIM:2 * HEAD_DIM] * inv_odd], axis=0).T.astype(BF16)
        abuf[qrows, c0:c0 + LANES] = out[:BLOCK]
        abuf[qrows, c0 + LANES:c0 + 2 * LANES] = out[BLOCK:]

    kbuf[:, 0:BLOCK, :] = kbuf[:, tm:tm + BLOCK, :]
    vbuf[:, 0:BLOCK] = vbuf[:, tm:tm + BLOCK]

    zbuf[HALO:HALO + tm, :] = z
    conv = _causal_conv3(zbuf, z, cmw_ref, tm, 0, D_MODEL)
    zbuf[HALO - 2:HALO, :] = zbuf[HALO - 2 + tm:HALO + tm, :]
    gated = (proj(OFF_CB, D_MODEL) * conv).astype(BF16)
    gate_conv = proj(OFF_GC, D_MODEL)
    gate_attn = proj(OFF_GA, D_MODEL)
    y_conv = _dot(gated, wco_ref[...])

    y_attn = _dot(abuf[...], wao_ref[...]) + bao_ref[...]

    y = x_prev + _dot(hbuf[...], wdn_ref[...])
    o_ref[0] = _rmsnorm(y, fin_ref[...]) if final_norm else y

    merged = jax.nn.sigmoid(gate_conv) * y_conv + jax.nn.sigmoid(gate_attn) * y_attn
    x1 = x + _dot(merged.astype(BF16), wmo_ref[...])
    x1buf[...] = x1
    u1buf[...] = _rmsnorm(x1, nfw_ref[...]).astype(BF16)


def _whole(space=pltpu.VMEM):
    return pl.BlockSpec(memory_space=space)


def kernel(x, positions, norm_mix_w, w_in, b_in, conv_mix_w, w_conv_out, w_attn_out, b_attn_out,
           sinks, w_mix_out, norm_ffn_w, w_ffn_up, ffn_conv_w, ffn_conv_b, w_ffn_down, norm_final_w):
    b, s, d = x.shape
    depth = w_in.shape[0]
    tm = TOKEN_TILE
    assert d == D_MODEL and s % tm == 0 and tm % BLOCK == 0
    nj = s // tm
    n_tiles = b * nj

    def in_tile(t):
        return jnp.minimum(t, n_tiles - 1)

    x_spec = pl.BlockSpec((1, tm, D_MODEL), lambda t: (in_tile(t), 0, 0))
    pos_spec = pl.BlockSpec((1, 1, tm), lambda t: (in_tile(t), 0, 0))
    out_spec = pl.BlockSpec((1, tm, D_MODEL), lambda t: (jnp.maximum(t - 1, 0), 0, 0))

    inv_freq = (ROPE_THETA ** (-jnp.arange(0, HEAD_DIM, 2, dtype=F32) / HEAD_DIM)).reshape(-1, 1)
    pos_tiles = positions.reshape(n_tiles, 1, tm)

    def rows(v):
        v = v.reshape(-1, v.shape[-1])
        return jnp.pad(v, ((0, 0), (0, IN_WIDTH - v.shape[-1])))

    def layer(final_norm):
        return pl.pallas_call(
            functools.partial(_block_kernel, nj, final_norm),
            grid=(n_tiles + 1,),
            in_specs=([_whole(pltpu.SMEM), pos_spec, x_spec] + [_whole()] * 2
                      + [pl.BlockSpec(memory_space=pl.ANY)] * 6),
            out_specs=out_spec,
            out_shape=jax.ShapeDtypeStruct((n_tiles, tm, D_MODEL), F32),
            scratch_shapes=[
                pltpu.VMEM((HALO + tm, D_MODEL), F32),
                pltpu.VMEM((4, BLOCK + tm, LANES), BF16),
                pltpu.VMEM((KV_WIDTH, BLOCK + tm), BF16),
                pltpu.VMEM((tm, Q_WIDTH), BF16),
                pltpu.VMEM((tm, Q_WIDTH), BF16),
                pltpu.VMEM((tm, D_MODEL), F32),
                pltpu.VMEM((tm, D_MODEL), BF16),
                pltpu.VMEM((HALO + tm, D_FF), F32),
                pltpu.VMEM((tm, D_FF), BF16),
                pltpu.VMEM((D_MODEL, IN_WIDTH), BF16),
                pltpu.VMEM((D_MODEL, 2 * D_FF), BF16),
                pltpu.VMEM((D_MODEL, D_MODEL), BF16),
                pltpu.VMEM((D_MODEL, D_MODEL), BF16),
                pltpu.VMEM((Q_WIDTH, D_MODEL), BF16),
                pltpu.VMEM((D_FF, D_MODEL), BF16),
                pltpu.VMEM((STAGE_SLOTS, STAGE_ROWS_WIDE, IN_WIDTH), F32),
                pltpu.VMEM((STAGE_SLOTS, STAGE_ROWS_NARROW, D_MODEL), F32),
                pltpu.SemaphoreType.DMA((STAGE_SLOTS,)),
            ],
            compiler_params=pltpu.CompilerParams(dimension_semantics=("arbitrary",),
                                                 vmem_limit_bytes=VMEM_LIMIT_BYTES),
            name="decoder_block",
        )

    x = x.reshape(n_tiles, tm, D_MODEL)
    for l in range(depth):
        params = jnp.concatenate(
            [rows(b_in[l]), rows(norm_mix_w[l]), rows(b_attn_out[l]), rows(norm_ffn_w[l]),
             rows(norm_final_w), rows(conv_mix_w[l]), rows(ffn_conv_w[l]), rows(ffn_conv_b[l])],
            axis=0)
        x = layer(l == depth - 1)(
            sinks[l], pos_tiles, x, inv_freq, params, w_in[l], w_ffn_up[l], w_conv_out[l],
            w_mix_out[l], w_attn_out[l], w_ffn_down[l])
    return x.reshape(b, s, D_MODEL)
```

```python
import functools

import jax
import jax.numpy as jnp
from jax import lax
from jax.experimental import pallas as pl
from jax.experimental.pallas import tpu as pltpu

D_MODEL = 1024
HEAD_DIM = 64
N_Q_HEADS = 8
N_KV_HEADS = 2
BLOCK = 128
ROPE_THETA = 10000.0
Q_WIDTH = N_Q_HEADS * HEAD_DIM
KV_WIDTH = N_KV_HEADS * HEAD_DIM
D_FF = 2816
EPS = 1e-5

OFF_CB = 0
OFF_CC = OFF_CB + D_MODEL
OFF_CX = OFF_CC + D_MODEL
OFF_Q = OFF_CX + D_MODEL
OFF_K = OFF_Q + Q_WIDTH
OFF_V = OFF_K + KV_WIDTH
OFF_GC = OFF_V + KV_WIDTH
OFF_GA = OFF_GC + D_MODEL
IN_WIDTH = OFF_GA + D_MODEL

ROW_B_IN, ROW_NORM_MIX, ROW_B_ATTN_OUT, ROW_NORM_FFN, ROW_NORM_FINAL = 0, 1, 2, 3, 4
ROW_CONV_MIX, ROW_FFN_CONV, ROW_FFN_CONV_B = 5, 8, 11

LANES = 128
HALO = 8
TOKEN_TILE = 256
FF_CHUNKS = ((0, 1024), (1024, 2048), (2048, D_FF))
VMEM_LIMIT_BYTES = 60 * 1024 * 1024

STAGE_SLOTS = 4
STAGE_ROWS_WIDE = 32
STAGE_ROWS_NARROW = 128

F32 = jnp.float32
BF16 = jnp.bfloat16


def _rmsnorm(x, w):
    ms = jnp.mean(x * x, axis=-1, keepdims=True)
    return x * lax.rsqrt(ms + EPS) * w


def _dot(a, b):
    return jnp.dot(a, b, preferred_element_type=F32)


def _dot_nt(a, b):
    return lax.dot_general(a, b, (((1,), (1,)), ((), ())), preferred_element_type=F32)


def _causal_conv3(buf, cur, taps_ref, tm, c0, c1):
    return (taps_ref[0:1, c0:c1] * buf[HALO - 2:HALO - 2 + tm, c0:c1]
            + taps_ref[1:2, c0:c1] * buf[HALO - 1:HALO - 1 + tm, c0:c1]
            + taps_ref[2:3, c0:c1] * cur)


def _load_as_bf16(src_hbm, dst_ref, stage, sems):
    rows, width = src_hbm.shape
    chunk = stage.shape[1]
    assert rows % chunk == 0 and width <= stage.shape[2]
    n = rows // chunk
    ahead = STAGE_SLOTS - 1

    def copy(i, slot):
        r0 = i * chunk if isinstance(i, int) else pl.multiple_of(i * chunk, chunk)
        return pltpu.make_async_copy(src_hbm.at[pl.ds(r0, chunk), :],
                                     stage.at[slot, :, 0:width], sems.at[slot])

    for i in range(min(ahead, n)):
        copy(i, i).start()

    def body(i, carry):
        slot = lax.rem(i, STAGE_SLOTS)
        copy(i, slot).wait()

        @pl.when(i + ahead < n)
        def _():
            copy(i + ahead, lax.rem(i + ahead, STAGE_SLOTS)).start()

        r0 = pl.multiple_of(i * chunk, chunk)
        dst_ref[pl.ds(r0, chunk), :] = stage[slot, :, 0:width].astype(BF16)
        return carry

    lax.fori_loop(0, n, body, 0)


def _block_kernel(tiles_per_seq, final_norm,
                  sinks_ref, pos_ref, x_ref, xnext_ref, invf_ref, prm_ref, win_hbm, wup_hbm,
                  wco_hbm, wmo_hbm, wao_hbm, wdn_hbm, o_ref,
                  zbuf, kbuf, vbuf, qbuf, abuf, x1buf, u1buf, umbuf, qkvbuf, ubuf, hbuf,
                  win_ref, wup_ref, wco_ref, wmo_ref, wao_ref, wdn_ref,
                  stage_wide, stage_narrow, stage_sems):
    t = pl.program_id(0)
    tm = x_ref.shape[1]
    mixer_seq_start = lax.rem(t, tiles_per_seq) == 0
    ffn_seq_start = lax.rem(t + tiles_per_seq - 1, tiles_per_seq) == 0

    bin_ref = prm_ref.at[ROW_B_IN:ROW_B_IN + 1, :]
    nmw_ref = prm_ref.at[ROW_NORM_MIX:ROW_NORM_MIX + 1, 0:D_MODEL]
    bao_ref = prm_ref.at[ROW_B_ATTN_OUT:ROW_B_ATTN_OUT + 1, 0:D_MODEL]
    nfw_ref = prm_ref.at[ROW_NORM_FFN:ROW_NORM_FFN + 1, 0:D_MODEL]
    fin_ref = prm_ref.at[ROW_NORM_FINAL:ROW_NORM_FINAL + 1, 0:D_MODEL]
    cmw_ref = prm_ref.at[ROW_CONV_MIX:ROW_CONV_MIX + 3, 0:D_MODEL]
    fcw_ref = prm_ref.at[ROW_FFN_CONV:ROW_FFN_CONV + 3, 0:D_FF]
    fcb_ref = prm_ref.at[ROW_FFN_CONV_B:ROW_FFN_CONV_B + 1, 0:D_FF]

    @pl.when(t == 0)
    def _():
        x1buf[...] = jnp.zeros(x1buf.shape, F32)
        u1buf[...] = jnp.zeros(u1buf.shape, BF16)
        _load_as_bf16(win_hbm, win_ref, stage_wide, stage_sems)
        _load_as_bf16(wup_hbm, wup_ref, stage_wide, stage_sems)
        _load_as_bf16(wco_hbm, wco_ref, stage_narrow, stage_sems)
        _load_as_bf16(wmo_hbm, wmo_ref, stage_narrow, stage_sems)
        _load_as_bf16(wao_hbm, wao_ref, stage_narrow, stage_sems)
        _load_as_bf16(wdn_hbm, wdn_ref, stage_narrow, stage_sems)
        u_first = _rmsnorm(x_ref[0], nmw_ref[...]).astype(BF16)
        umbuf[...] = u_first
        qkvbuf[...] = (_dot(u_first, win_ref[:, OFF_Q:OFF_GC]) + bin_ref[:, OFF_Q:OFF_GC])

    @pl.when(mixer_seq_start)
    def _():
        zbuf[0:HALO, :] = jnp.zeros((HALO, D_MODEL), F32)
        kbuf[:, 0:BLOCK, :] = jnp.zeros((4, BLOCK, LANES), BF16)
        vbuf[:, 0:BLOCK] = jnp.zeros((KV_WIDTH, BLOCK), BF16)

    @pl.when(jnp.logical_or(ffn_seq_start, t == 0))
    def _():
        ubuf[0:HALO, :] = jnp.zeros((HALO, D_FF), F32)

    x_prev = x1buf[...]
    u_prev = u1buf[...]
    x = x_ref[0]
    u = umbuf[...]

    def proj_of(lhs, off, width):
        return _dot(lhs, win_ref[:, off:off + width]) + bin_ref[:, off:off + width]

    def proj(off, width):
        return proj_of(u, off, width)

    def ffn_up(c0, c1):
        up = _dot(u_prev, wup_ref[:, c0:c1])
        ubuf[HALO:HALO + tm, c0:c1] = up
        a = _causal_conv3(ubuf, up, fcw_ref, tm, c0, c1) + fcb_ref[:, c0:c1]
        gate = _dot(u_prev, wup_ref[:, D_FF + c0:D_FF + c1])
        hbuf[:, c0:c1] = (jax.nn.silu(a) * gate).astype(BF16)

    qkv = qkvbuf[...]
    ffn_up(*FF_CHUNKS[0])
    ffn_up(*FF_CHUNKS[1])

    ang = invf_ref[...] * pos_ref[0].astype(F32)
    cos_t, sin_t = jnp.cos(ang), jnp.sin(ang)
    cos = jnp.concatenate([cos_t] * 4, axis=0).T
    sin = jnp.concatenate([-sin_t, sin_t, -sin_t, sin_t], axis=0).T

    lane = lax.broadcasted_iota(jnp.int32, (tm, LANES), 1)
    first_half = (lane & (HEAD_DIM // 2)) == 0
    low_head = lane < HEAD_DIM

    def rope(v, c, s):
        partner = jnp.where(first_half, pltpu.roll(v, LANES - HEAD_DIM // 2, 1),
                            pltpu.roll(v, HEAD_DIM // 2, 1))
        return v * c + partner * s

    scale = HEAD_DIM ** -0.5
    cos_q, sin_q = cos * scale, sin * scale
    for c in range(Q_WIDTH // LANES):
        sl = slice(c * LANES, (c + 1) * LANES)
        qbuf[:, sl] = rope(qkv[:, sl], cos_q, sin_q).astype(BF16)

    k = rope(qkv[:, Q_WIDTH:Q_WIDTH + KV_WIDTH], cos, sin)
    krot = pltpu.roll(k, HEAD_DIM, 1)
    kzero = jnp.zeros_like(k)
    rows = slice(BLOCK, BLOCK + tm)
    kbuf[0, rows, :] = jnp.where(low_head, k, kzero).astype(BF16)
    kbuf[1, rows, :] = jnp.where(low_head, kzero, krot).astype(BF16)
    kbuf[2, rows, :] = jnp.where(low_head, krot, kzero).astype(BF16)
    kbuf[3, rows, :] = jnp.where(low_head, kzero, k).astype(BF16)
    vbuf[:, BLOCK:BLOCK + tm] = qkv[:, Q_WIDTH + KV_WIDTH:].T.astype(BF16)

    key2 = lax.broadcasted_iota(jnp.int32, (2 * BLOCK, 2 * BLOCK), 0)
    qry2 = lax.broadcasted_iota(jnp.int32, (2 * BLOCK, 2 * BLOCK), 1) & (BLOCK - 1)
    band = (key2 > qry2) & (key2 <= qry2 + BLOCK)
    neg_inf = jnp.full((2 * BLOCK, 2 * BLOCK), -jnp.inf, F32)
    zeros2 = jnp.zeros((2 * BLOCK, 2 * BLOCK), F32)
    bias = jnp.where(band, zeros2, neg_inf)
    bias_seq_start = jnp.where(band & (key2 >= BLOCK), zeros2, neg_inf)
    bias_first = jnp.where(mixer_seq_start, bias_seq_start, bias)

    z = proj(OFF_CC, D_MODEL) * proj(OFF_CX, D_MODEL)

    blocks = [(i, g) for i in range(tm // BLOCK) for g in range(N_KV_HEADS)]
    scores = {}
    for i, g in blocks:
        qrows = slice(i * BLOCK, (i + 1) * BLOCK)
        krows = slice(i * BLOCK, (i + 2) * BLOCK)
        blk_bias = bias_first if i == 0 else bias
        c0 = g * 2 * LANES
        qpair = jnp.concatenate([qbuf[qrows, c0:c0 + LANES],
                                 qbuf[qrows, c0 + LANES:c0 + 2 * LANES]], axis=0)
        scores[i, g] = (_dot_nt(kbuf[2 * g, krows, :], qpair) + blk_bias,
                        _dot_nt(kbuf[2 * g + 1, krows, :], qpair) + blk_bias)

    ffn_up(*FF_CHUNKS[2])
    ubuf[HALO - 2:HALO, :] = ubuf[HALO - 2 + tm:HALO + tm, :]

    top_pair = lax.broadcasted_iota(jnp.int32, (1, 2 * BLOCK), 1) < BLOCK
    sum_row = lax.broadcasted_iota(jnp.int32, (2 * HALO, 4 * BLOCK), 0)
    sum_col = lax.broadcasted_iota(jnp.int32, (2 * HALO, 4 * BLOCK), 1)
    sum_rows = jnp.where(sum_row == (sum_col >= 2 * BLOCK).astype(jnp.int32), 1.0, 0.0)
    sum_rows = sum_rows.astype(BF16)
    vzero = jnp.zeros((HEAD_DIM, 2 * BLOCK), BF16)

    for i, g in blocks:
        qrows = slice(i * BLOCK, (i + 1) * BLOCK)
        kcols = slice(i * BLOCK, (i + 2) * BLOCK)
        c0 = g * 2 * LANES
        s_even, s_odd = scores[i, g]
        sink_even = jnp.where(top_pair, sinks_ref[4 * g], sinks_ref[4 * g + 2])
        sink_odd = jnp.where(top_pair, sinks_ref[4 * g + 1], sinks_ref[4 * g + 3])
        m_even = jnp.maximum(jnp.max(s_even, axis=0, keepdims=True), sink_even)
        m_odd = jnp.maximum(jnp.max(s_odd, axis=0, keepdims=True), sink_odd)
        p_t = jnp.concatenate([jnp.exp(s_even - m_even).astype(BF16),
                               jnp.exp(s_odd - m_odd).astype(BF16)], axis=0)
        v_t = vbuf[g * HEAD_DIM:(g + 1) * HEAD_DIM, kcols]
        lhs = jnp.concatenate([jnp.concatenate([v_t, vzero], axis=1),
                               jnp.concatenate([vzero, v_t], axis=1), sum_rows], axis=0)
        r = _dot(lhs, p_t)
        inv_even = 1.0 / (r[2 * HEAD_DIM:2 * HEAD_DIM + 1] + jnp.exp(sink_even - m_even))
        inv_odd = 1.0 / (r[2 * HEAD_DIM + 1:2 * HEAD_DIM + 2] + jnp.exp(sink_odd - m_odd))
        out = jnp.concatenate([r[0:HEAD_DIM] * inv_even,
                               r[HEAD_DIM:2 * HEAD_DIM] * inv_odd], axis=0).T.astype(BF16)
        abuf[qrows, c0:c0 + LANES] = out[:BLOCK]
        abuf[qrows, c0 + LANES:c0 + 2 * LANES] = out[BLOCK:]

    kbuf[:, 0:BLOCK, :] = kbuf[:, tm:tm + BLOCK, :]
    vbuf[:, 0:BLOCK] = vbuf[:, tm:tm + BLOCK]

    zbuf[HALO:HALO + tm, :] = z
    conv = _causal_conv3(zbuf, z, cmw_ref, tm, 0, D_MODEL)
    zbuf[HALO - 2:HALO, :] = zbuf[HALO - 2 + tm:HALO + tm, :]
    gated = (proj(OFF_CB, D_MODEL) * conv).astype(BF16)
    gate_conv = proj(OFF_GC, D_MODEL)
    gate_attn = proj(OFF_GA, D_MODEL)
    y_conv = _dot(gated, wco_ref[...])

    y_attn = _dot(abuf[...], wao_ref[...]) + bao_ref[...]

    y = x_prev + _dot(hbuf[...], wdn_ref[...])
    o_ref[0] = _rmsnorm(y, fin_ref[...]) if final_norm else y

    merged = jax.nn.sigmoid(gate_conv) * y_conv + jax.nn.sigmoid(gate_attn) * y_attn
    x1 = x + _dot(merged.astype(BF16), wmo_ref[...])
    x1buf[...] = x1
    u1buf[...] = _rmsnorm(x1, nfw_ref[...]).astype(BF16)

    u_next = _rmsnorm(xnext_ref[0], nmw_ref[...]).astype(BF16)
    umbuf[...] = u_next
    qkvbuf[...] = proj_of(u_next, OFF_Q, Q_WIDTH + 2 * KV_WIDTH)


def _whole(space=pltpu.VMEM):
    return pl.BlockSpec(memory_space=space)


def kernel(x, positions, norm_mix_w, w_in, b_in, conv_mix_w, w_conv_out, w_attn_out, b_attn_out,
           sinks, w_mix_out, norm_ffn_w, w_ffn_up, ffn_conv_w, ffn_conv_b, w_ffn_down, norm_final_w):
    b, s, d = x.shape
    depth = w_in.shape[0]
    tm = TOKEN_TILE
    assert d == D_MODEL and s % tm == 0 and tm % BLOCK == 0
    nj = s // tm
    n_tiles = b * nj

    def in_tile(t):
        return jnp.minimum(t, n_tiles - 1)

    x_spec = pl.BlockSpec((1, tm, D_MODEL), lambda t: (in_tile(t), 0, 0))
    xnext_spec = pl.BlockSpec((1, tm, D_MODEL), lambda t: (in_tile(t + 1), 0, 0))
    pos_spec = pl.BlockSpec((1, 1, tm), lambda t: (in_tile(t), 0, 0))
    out_spec = pl.BlockSpec((1, tm, D_MODEL), lambda t: (jnp.maximum(t - 1, 0), 0, 0))

    inv_freq = (ROPE_THETA ** (-jnp.arange(0, HEAD_DIM, 2, dtype=F32) / HEAD_DIM)).reshape(-1, 1)
    pos_tiles = positions.reshape(n_tiles, 1, tm)

    def rows(v):
        v = v.reshape(-1, v.shape[-1])
        return jnp.pad(v, ((0, 0), (0, IN_WIDTH - v.shape[-1])))

    def layer(final_norm):
        return pl.pallas_call(
            functools.partial(_block_kernel, nj, final_norm),
            grid=(n_tiles + 1,),
            in_specs=([_whole(pltpu.SMEM), pos_spec, x_spec, xnext_spec] + [_whole()] * 2
                      + [pl.BlockSpec(memory_space=pl.ANY)] * 6),
            out_specs=out_spec,
            out_shape=jax.ShapeDtypeStruct((n_tiles, tm, D_MODEL), F32),
            scratch_shapes=[
                pltpu.VMEM((HALO + tm, D_MODEL), F32),
                pltpu.VMEM((4, BLOCK + tm, LANES), BF16),
                pltpu.VMEM((KV_WIDTH, BLOCK + tm), BF16),
                pltpu.VMEM((tm, Q_WIDTH), BF16),
                pltpu.VMEM((tm, Q_WIDTH), BF16),
                pltpu.VMEM((tm, D_MODEL), F32),
                pltpu.VMEM((tm, D_MODEL), BF16),
                pltpu.VMEM((tm, D_MODEL), BF16),
                pltpu.VMEM((tm, Q_WIDTH + 2 * KV_WIDTH), F32),
                pltpu.VMEM((HALO + tm, D_FF), F32),
                pltpu.VMEM((tm, D_FF), BF16),
                pltpu.VMEM((D_MODEL, IN_WIDTH), BF16),
                pltpu.VMEM((D_MODEL, 2 * D_FF), BF16),
                pltpu.VMEM((D_MODEL, D_MODEL), BF16),
                pltpu.VMEM((D_MODEL, D_MODEL), BF16),
                pltpu.VMEM((Q_WIDTH, D_MODEL), BF16),
                pltpu.VMEM((D_FF, D_MODEL), BF16),
                pltpu.VMEM((STAGE_SLOTS, STAGE_ROWS_WIDE, IN_WIDTH), F32),
                pltpu.VMEM((STAGE_SLOTS, STAGE_ROWS_NARROW, D_MODEL), F32),
                pltpu.SemaphoreType.DMA((STAGE_SLOTS,)),
            ],
            compiler_params=pltpu.CompilerParams(dimension_semantics=("arbitrary",),
                                                 vmem_limit_bytes=VMEM_LIMIT_BYTES),
            name="decoder_block",
        )

    x = x.reshape(n_tiles, tm, D_MODEL)
    for l in range(depth):
        params = jnp.concatenate(
            [rows(b_in[l]), rows(norm_mix_w[l]), rows(b_attn_out[l]), rows(norm_ffn_w[l]),
             rows(norm_final_w), rows(conv_mix_w[l]), rows(ffn_conv_w[l]), rows(ffn_conv_b[l])],
            axis=0)
        x = layer(l == depth - 1)(
            sinks[l], pos_tiles, x, x, inv_freq, params, w_in[l], w_ffn_up[l], w_conv_out[l],
            w_mix_out[l], w_attn_out[l], w_ffn_down[l])
    return x.reshape(b, s, D_MODEL)
```

```python
import functools

import jax
import jax.numpy as jnp
from jax import lax
from jax.experimental import pallas as pl
from jax.experimental.pallas import tpu as pltpu

D_MODEL = 1024
HEAD_DIM = 64
N_Q_HEADS = 8
N_KV_HEADS = 2
BLOCK = 128
ROPE_THETA = 10000.0
Q_WIDTH = N_Q_HEADS * HEAD_DIM
KV_WIDTH = N_KV_HEADS * HEAD_DIM
D_FF = 2816
EPS = 1e-5

OFF_CB = 0
OFF_CC = OFF_CB + D_MODEL
OFF_CX = OFF_CC + D_MODEL
OFF_Q = OFF_CX + D_MODEL
OFF_K = OFF_Q + Q_WIDTH
OFF_V = OFF_K + KV_WIDTH
OFF_GC = OFF_V + KV_WIDTH
OFF_GA = OFF_GC + D_MODEL
IN_WIDTH = OFF_GA + D_MODEL

ROW_B_IN, ROW_NORM_MIX, ROW_B_ATTN_OUT, ROW_NORM_FFN, ROW_NORM_FINAL = 0, 1, 2, 3, 4
ROW_CONV_MIX, ROW_FFN_CONV, ROW_FFN_CONV_B = 5, 8, 11

LANES = 128
HALO = 8
TOKEN_TILE = 256
FF_CHUNKS = ((0, 1024), (1024, 2048), (2048, D_FF))
VMEM_LIMIT_BYTES = 60 * 1024 * 1024

STAGE_SLOTS = 4
STAGE_ROWS_WIDE = 32
STAGE_ROWS_NARROW = 128

F32 = jnp.float32
BF16 = jnp.bfloat16


def _rmsnorm(x, w):
    ms = jnp.mean(x * x, axis=-1, keepdims=True)
    return x * lax.rsqrt(ms + EPS) * w


def _dot(a, b):
    return jnp.dot(a, b, preferred_element_type=F32)


def _dot_nt(a, b):
    return lax.dot_general(a, b, (((1,), (1,)), ((), ())), preferred_element_type=F32)


def _causal_conv3(halo_ref, cur, taps_ref, c0, c1):
    tm = cur.shape[0]
    first_row = lax.broadcasted_iota(jnp.int32, (HALO, c1 - c0), 0) == 0

    def shift_down(v, halo_row):
        rolled = pltpu.roll(v, 1, 0)
        head = jnp.where(first_row, halo_row, rolled[0:HALO])
        return jnp.concatenate([head, rolled[HALO:]], axis=0)

    prev1 = shift_down(cur, halo_ref[HALO - 1:HALO, c0:c1])
    prev2 = shift_down(prev1, halo_ref[HALO - 2:HALO - 1, c0:c1])
    halo_ref[HALO - 2:HALO, c0:c1] = cur[tm - 2:tm]
    return (taps_ref[0:1, c0:c1] * prev2 + taps_ref[1:2, c0:c1] * prev1
            + taps_ref[2:3, c0:c1] * cur)


def _load_as_bf16(src_hbm, dst_ref, stage, sems):
    rows, width = src_hbm.shape
    chunk = stage.shape[1]
    assert rows % chunk == 0 and width <= stage.shape[2]
    n = rows // chunk
    ahead = STAGE_SLOTS - 1

    def copy(i, slot):
        r0 = i * chunk if isinstance(i, int) else pl.multiple_of(i * chunk, chunk)
        return pltpu.make_async_copy(src_hbm.at[pl.ds(r0, chunk), :],
                                     stage.at[slot, :, 0:width], sems.at[slot])

    for i in range(min(ahead, n)):
        copy(i, i).start()

    def body(i, carry):
        slot = lax.rem(i, STAGE_SLOTS)
        copy(i, slot).wait()

        @pl.when(i + ahead < n)
        def _():
            copy(i + ahead, lax.rem(i + ahead, STAGE_SLOTS)).start()

        r0 = pl.multiple_of(i * chunk, chunk)
        dst_ref[pl.ds(r0, chunk), :] = stage[slot, :, 0:width].astype(BF16)
        return carry

    lax.fori_loop(0, n, body, 0)


def _block_kernel(tiles_per_seq, final_norm,
                  sinks_ref, pos_ref, x_ref, xnext_ref, invf_ref, prm_ref, win_hbm, wup_hbm,
                  wco_hbm, wmo_hbm, wao_hbm, wdn_hbm, o_ref,
                  zbuf, kbuf, vbuf, qbuf, abuf, x1buf, u1buf, umbuf, qkvbuf, ubuf, hbuf,
                  win_ref, wup_ref, wco_ref, wmo_ref, wao_ref, wdn_ref,
                  stage_wide, stage_narrow, stage_sems):
    t = pl.program_id(0)
    tm = x_ref.shape[1]
    mixer_seq_start = lax.rem(t, tiles_per_seq) == 0
    ffn_seq_start = lax.rem(t + tiles_per_seq - 1, tiles_per_seq) == 0

    bin_ref = prm_ref.at[ROW_B_IN:ROW_B_IN + 1, :]
    nmw_ref = prm_ref.at[ROW_NORM_MIX:ROW_NORM_MIX + 1, 0:D_MODEL]
    bao_ref = prm_ref.at[ROW_B_ATTN_OUT:ROW_B_ATTN_OUT + 1, 0:D_MODEL]
    nfw_ref = prm_ref.at[ROW_NORM_FFN:ROW_NORM_FFN + 1, 0:D_MODEL]
    fin_ref = prm_ref.at[ROW_NORM_FINAL:ROW_NORM_FINAL + 1, 0:D_MODEL]
    cmw_ref = prm_ref.at[ROW_CONV_MIX:ROW_CONV_MIX + 3, 0:D_MODEL]
    fcw_ref = prm_ref.at[ROW_FFN_CONV:ROW_FFN_CONV + 3, 0:D_FF]
    fcb_ref = prm_ref.at[ROW_FFN_CONV_B:ROW_FFN_CONV_B + 1, 0:D_FF]

    @pl.when(t == 0)
    def _():
        x1buf[...] = jnp.zeros(x1buf.shape, F32)
        u1buf[...] = jnp.zeros(u1buf.shape, BF16)
        _load_as_bf16(win_hbm, win_ref, stage_wide, stage_sems)
        _load_as_bf16(wup_hbm, wup_ref, stage_wide, stage_sems)
        _load_as_bf16(wco_hbm, wco_ref, stage_narrow, stage_sems)
        _load_as_bf16(wmo_hbm, wmo_ref, stage_narrow, stage_sems)
        _load_as_bf16(wao_hbm, wao_ref, stage_narrow, stage_sems)
        _load_as_bf16(wdn_hbm, wdn_ref, stage_narrow, stage_sems)
        u_first = _rmsnorm(x_ref[0], nmw_ref[...]).astype(BF16)
        umbuf[...] = u_first
        qkvbuf[...] = (_dot(u_first, win_ref[:, OFF_Q:OFF_GC]) + bin_ref[:, OFF_Q:OFF_GC])

    @pl.when(mixer_seq_start)
    def _():
        zbuf[0:HALO, :] = jnp.zeros((HALO, D_MODEL), F32)
        kbuf[:, 0:BLOCK, :] = jnp.zeros((4, BLOCK, LANES), BF16)
        vbuf[:, 0:BLOCK] = jnp.zeros((KV_WIDTH, BLOCK), BF16)

    @pl.when(jnp.logical_or(ffn_seq_start, t == 0))
    def _():
        ubuf[0:HALO, :] = jnp.zeros((HALO, D_FF), F32)

    x_prev = x1buf[...]
    u_prev = u1buf[...]
    x = x_ref[0]
    u = umbuf[...]

    def proj_of(lhs, off, width):
        return _dot(lhs, win_ref[:, off:off + width]) + bin_ref[:, off:off + width]

    def proj(off, width):
        return proj_of(u, off, width)

    def ffn_up(c0, c1):
        up = _dot(u_prev, wup_ref[:, c0:c1])
        a = _causal_conv3(ubuf, up, fcw_ref, c0, c1) + fcb_ref[:, c0:c1]
        gate = _dot(u_prev, wup_ref[:, D_FF + c0:D_FF + c1])
        hbuf[:, c0:c1] = (jax.nn.silu(a) * gate).astype(BF16)

    qkv = qkvbuf[...]
    ffn_up(*FF_CHUNKS[0])
    ffn_up(*FF_CHUNKS[1])

    ang = invf_ref[...] * pos_ref[0].astype(F32)
    cos_t, sin_t = jnp.cos(ang), jnp.sin(ang)
    cos = jnp.concatenate([cos_t] * 4, axis=0).T
    sin = jnp.concatenate([-sin_t, sin_t, -sin_t, sin_t], axis=0).T

    lane = lax.broadcasted_iota(jnp.int32, (tm, LANES), 1)
    first_half = (lane & (HEAD_DIM // 2)) == 0
    low_head = lane < HEAD_DIM

    def rope(v, c, s):
        partner = jnp.where(first_half, pltpu.roll(v, LANES - HEAD_DIM // 2, 1),
                            pltpu.roll(v, HEAD_DIM // 2, 1))
        return v * c + partner * s

    scale = HEAD_DIM ** -0.5
    cos_q, sin_q = cos * scale, sin * scale
    for c in range(Q_WIDTH // LANES):
        sl = slice(c * LANES, (c + 1) * LANES)
        qbuf[:, sl] = rope(qkv[:, sl], cos_q, sin_q).astype(BF16)

    k = rope(qkv[:, Q_WIDTH:Q_WIDTH + KV_WIDTH], cos, sin)
    krot = pltpu.roll(k, HEAD_DIM, 1)
    kzero = jnp.zeros_like(k)
    rows = slice(BLOCK, BLOCK + tm)
    kbuf[0, rows, :] = jnp.where(low_head, k, kzero).astype(BF16)
    kbuf[1, rows, :] = jnp.where(low_head, kzero, krot).astype(BF16)
    kbuf[2, rows, :] = jnp.where(low_head, krot, kzero).astype(BF16)
    kbuf[3, rows, :] = jnp.where(low_head, kzero, k).astype(BF16)
    vbuf[:, BLOCK:BLOCK + tm] = qkv[:, Q_WIDTH + KV_WIDTH:].T.astype(BF16)

    key2 = lax.broadcasted_iota(jnp.int32, (2 * BLOCK, 2 * BLOCK), 0)
    qry2 = lax.broadcasted_iota(jnp.int32, (2 * BLOCK, 2 * BLOCK), 1) & (BLOCK - 1)
    band = (key2 > qry2) & (key2 <= qry2 + BLOCK)
    neg_inf = jnp.full((2 * BLOCK, 2 * BLOCK), -jnp.inf, F32)
    zeros2 = jnp.zeros((2 * BLOCK, 2 * BLOCK), F32)
    bias = jnp.where(band, zeros2, neg_inf)
    bias_seq_start = jnp.where(band & (key2 >= BLOCK), zeros2, neg_inf)
    bias_first = jnp.where(mixer_seq_start, bias_seq_start, bias)

    z = proj(OFF_CC, D_MODEL) * proj(OFF_CX, D_MODEL)

    blocks = [(i, g) for i in range(tm // BLOCK) for g in range(N_KV_HEADS)]
    scores = {}
    for i, g in blocks:
        qrows = slice(i * BLOCK, (i + 1) * BLOCK)
        krows = slice(i * BLOCK, (i + 2) * BLOCK)
        blk_bias = bias_first if i == 0 else bias
        c0 = g * 2 * LANES
        qpair = jnp.concatenate([qbuf[qrows, c0:c0 + LANES],
                                 qbuf[qrows, c0 + LANES:c0 + 2 * LANES]], axis=0)
        scores[i, g] = (_dot_nt(kbuf[2 * g, krows, :], qpair) + blk_bias,
                        _dot_nt(kbuf[2 * g + 1, krows, :], qpair) + blk_bias)

    ffn_up(*FF_CHUNKS[2])

    top_pair = lax.broadcasted_iota(jnp.int32, (1, 2 * BLOCK), 1) < BLOCK
    sum_row = lax.broadcasted_iota(jnp.int32, (2 * HALO, 4 * BLOCK), 0)
    sum_col = lax.broadcasted_iota(jnp.int32, (2 * HALO, 4 * BLOCK), 1)
    sum_rows = jnp.where(sum_row == (sum_col >= 2 * BLOCK).astype(jnp.int32), 1.0, 0.0)
    sum_rows = sum_rows.astype(BF16)
    vzero = jnp.zeros((HEAD_DIM, 2 * BLOCK), BF16)

    for i, g in blocks:
        qrows = slice(i * BLOCK, (i + 1) * BLOCK)
        kcols = slice(i * BLOCK, (i + 2) * BLOCK)
        c0 = g * 2 * LANES
        s_even, s_odd = scores[i, g]
        sink_even = jnp.where(top_pair, sinks_ref[4 * g], sinks_ref[4 * g + 2])
        sink_odd = jnp.where(top_pair, sinks_ref[4 * g + 1], sinks_ref[4 * g + 3])
        m_even = jnp.maximum(jnp.max(s_even, axis=0, keepdims=True), sink_even)
        m_odd = jnp.maximum(jnp.max(s_odd, axis=0, keepdims=True), sink_odd)
        p_t = jnp.concatenate([jnp.exp(s_even - m_even).astype(BF16),
                               jnp.exp(s_odd - m_odd).astype(BF16)], axis=0)
        v_t = vbuf[g * HEAD_DIM:(g + 1) * HEAD_DIM, kcols]
        lhs = jnp.concatenate([jnp.concatenate([v_t, vzero], axis=1),
                               jnp.concatenate([vzero, v_t], axis=1), sum_rows], axis=0)
        r = _dot(lhs, p_t)
        inv_even = 1.0 / (r[2 * HEAD_DIM:2 * HEAD_DIM + 1] + jnp.exp(sink_even - m_even))
        inv_odd = 1.0 / (r[2 * HEAD_DIM + 1:2 * HEAD_DIM + 2] + jnp.exp(sink_odd - m_odd))
        out = jnp.concatenate([r[0:HEAD_DIM] * inv_even,
                               r[HEAD_DIM:2 * HEAD_DIM] * inv_odd], axis=0).T.astype(BF16)
        abuf[qrows, c0:c0 + LANES] = out[:BLOCK]
        abuf[qrows, c0 + LANES:c0 + 2 * LANES] = out[BLOCK:]

    kbuf[:, 0:BLOCK, :] = kbuf[:, tm:tm + BLOCK, :]
    vbuf[:, 0:BLOCK] = vbuf[:, tm:tm + BLOCK]

    conv = _causal_conv3(zbuf, z, cmw_ref, 0, D_MODEL)
    gated = (proj(OFF_CB, D_MODEL) * conv).astype(BF16)
    gate_conv = proj(OFF_GC, D_MODEL)
    gate_attn = proj(OFF_GA, D_MODEL)
    y_conv = _dot(gated, wco_ref[...])

    y_attn = _dot(abuf[...], wao_ref[...]) + bao_ref[...]

    y = x_prev + _dot(hbuf[...], wdn_ref[...])
    o_ref[0] = _rmsnorm(y, fin_ref[...]) if final_norm else y

    merged = jax.nn.sigmoid(gate_conv) * y_conv + jax.nn.sigmoid(gate_attn) * y_attn
    x1 = x + _dot(merged.astype(BF16), wmo_ref[...])
    x1buf[...] = x1
    u1buf[...] = _rmsnorm(x1, nfw_ref[...]).astype(BF16)

    u_next = _rmsnorm(xnext_ref[0], nmw_ref[...]).astype(BF16)
    umbuf[...] = u_next
    qkvbuf[...] = proj_of(u_next, OFF_Q, Q_WIDTH + 2 * KV_WIDTH)


def _whole(space=pltpu.VMEM):
    return pl.BlockSpec(memory_space=space)


def kernel(x, positions, norm_mix_w, w_in, b_in, conv_mix_w, w_conv_out, w_attn_out, b_attn_out,
           sinks, w_mix_out, norm_ffn_w, w_ffn_up, ffn_conv_w, ffn_conv_b, w_ffn_down, norm_final_w):
    b, s, d = x.shape
    depth = w_in.shape[0]
    tm = TOKEN_TILE
    assert d == D_MODEL and s % tm == 0 and tm % BLOCK == 0
    nj = s // tm
    n_tiles = b * nj

    def in_tile(t):
        return jnp.minimum(t, n_tiles - 1)

    x_spec = pl.BlockSpec((1, tm, D_MODEL), lambda t: (in_tile(t), 0, 0))
    xnext_spec = pl.BlockSpec((1, tm, D_MODEL), lambda t: (in_tile(t + 1), 0, 0))
    pos_spec = pl.BlockSpec((1, 1, tm), lambda t: (in_tile(t), 0, 0))
    out_spec = pl.BlockSpec((1, tm, D_MODEL), lambda t: (jnp.maximum(t - 1, 0), 0, 0))

    inv_freq = (ROPE_THETA ** (-jnp.arange(0, HEAD_DIM, 2, dtype=F32) / HEAD_DIM)).reshape(-1, 1)
    pos_tiles = positions.reshape(n_tiles, 1, tm)

    def rows(v):
        v = v.reshape(-1, v.shape[-1])
        return jnp.pad(v, ((0, 0), (0, IN_WIDTH - v.shape[-1])))

    def layer(final_norm):
        return pl.pallas_call(
            functools.partial(_block_kernel, nj, final_norm),
            grid=(n_tiles + 1,),
            in_specs=([_whole(pltpu.SMEM), pos_spec, x_spec, xnext_spec] + [_whole()] * 2
                      + [pl.BlockSpec(memory_space=pl.ANY)] * 6),
            out_specs=out_spec,
            out_shape=jax.ShapeDtypeStruct((n_tiles, tm, D_MODEL), F32),
            scratch_shapes=[
                pltpu.VMEM((HALO, D_MODEL), F32),
                pltpu.VMEM((4, BLOCK + tm, LANES), BF16),
                pltpu.VMEM((KV_WIDTH, BLOCK + tm), BF16),
                pltpu.VMEM((tm, Q_WIDTH), BF16),
                pltpu.VMEM((tm, Q_WIDTH), BF16),
                pltpu.VMEM((tm, D_MODEL), F32),
                pltpu.VMEM((tm, D_MODEL), BF16),
                pltpu.VMEM((tm, D_MODEL), BF16),
                pltpu.VMEM((tm, Q_WIDTH + 2 * KV_WIDTH), F32),
                pltpu.VMEM((HALO, D_FF), F32),
                pltpu.VMEM((tm, D_FF), BF16),
                pltpu.VMEM((D_MODEL, IN_WIDTH), BF16),
                pltpu.VMEM((D_MODEL, 2 * D_FF), BF16),
                pltpu.VMEM((D_MODEL, D_MODEL), BF16),
                pltpu.VMEM((D_MODEL, D_MODEL), BF16),
                pltpu.VMEM((Q_WIDTH, D_MODEL), BF16),
                pltpu.VMEM((D_FF, D_MODEL), BF16),
                pltpu.VMEM((STAGE_SLOTS, STAGE_ROWS_WIDE, IN_WIDTH), F32),
                pltpu.VMEM((STAGE_SLOTS, STAGE_ROWS_NARROW, D_MODEL), F32),
                pltpu.SemaphoreType.DMA((STAGE_SLOTS,)),
            ],
            compiler_params=pltpu.CompilerParams(dimension_semantics=("arbitrary",),
                                                 vmem_limit_bytes=VMEM_LIMIT_BYTES),
            name="decoder_block",
        )

    x = x.reshape(n_tiles, tm, D_MODEL)
    for l in range(depth):
        params = jnp.concatenate(
            [rows(b_in[l]), rows(norm_mix_w[l]), rows(b_attn_out[l]), rows(norm_ffn_w[l]),
             rows(norm_final_w), rows(conv_mix_w[l]), rows(ffn_conv_w[l]), rows(ffn_conv_b[l])],
            axis=0)
        x = layer(l == depth - 1)(
            sinks[l], pos_tiles, x, x, inv_freq, params, w_in[l], w_ffn_up[l], w_conv_out[l],
            w_mix_out[l], w_attn_out[l], w_ffn_down[l])
    return x.reshape(b, s, D_MODEL)
```

```python
import functools

import jax
import jax.numpy as jnp
from jax import lax
from jax.experimental import pallas as pl
from jax.experimental.pallas import tpu as pltpu

D_MODEL = 1024
HEAD_DIM = 64
N_Q_HEADS = 8
N_KV_HEADS = 2
BLOCK = 128
ROPE_THETA = 10000.0
Q_WIDTH = N_Q_HEADS * HEAD_DIM
KV_WIDTH = N_KV_HEADS * HEAD_DIM
D_FF = 2816
EPS = 1e-5

OFF_CB = 0
OFF_CC = OFF_CB + D_MODEL
OFF_CX = OFF_CC + D_MODEL
OFF_Q = OFF_CX + D_MODEL
OFF_K = OFF_Q + Q_WIDTH
OFF_V = OFF_K + KV_WIDTH
OFF_GC = OFF_V + KV_WIDTH
OFF_GA = OFF_GC + D_MODEL
IN_WIDTH = OFF_GA + D_MODEL

ROW_B_IN, ROW_NORM_MIX, ROW_B_ATTN_OUT, ROW_NORM_FFN, ROW_NORM_FINAL = 0, 1, 2, 3, 4
ROW_CONV_MIX, ROW_FFN_CONV, ROW_FFN_CONV_B = 5, 8, 11

LANES = 128
HALO = 8
TOKEN_TILE = 256
FF_CHUNKS = ((0, 1024), (1024, 2048), (2048, D_FF))
VMEM_LIMIT_BYTES = 60 * 1024 * 1024

STAGE_SLOTS = 4
STAGE_ROWS_WIDE = 32
STAGE_ROWS_NARROW = 128

F32 = jnp.float32
BF16 = jnp.bfloat16


def _rmsnorm(x, w):
    ms = jnp.mean(x * x, axis=-1, keepdims=True)
    return x * lax.rsqrt(ms + EPS) * w


def _dot(a, b):
    return jnp.dot(a, b, preferred_element_type=F32)


def _dot_nt(a, b):
    return lax.dot_general(a, b, (((1,), (1,)), ((), ())), preferred_element_type=F32)


def _causal_conv3(halo_ref, cur, taps_ref, c0, c1):
    tm = cur.shape[0]
    first_row = lax.broadcasted_iota(jnp.int32, (HALO, c1 - c0), 0) == 0

    def shift_down(v, halo_row):
        rolled = pltpu.roll(v, 1, 0)
        head = jnp.where(first_row, halo_row, rolled[0:HALO])
        return jnp.concatenate([head, rolled[HALO:]], axis=0)

    prev1 = shift_down(cur, halo_ref[HALO - 1:HALO, c0:c1])
    prev2 = shift_down(prev1, halo_ref[HALO - 2:HALO - 1, c0:c1])
    halo_ref[HALO - 2:HALO, c0:c1] = cur[tm - 2:tm]
    return (taps_ref[0:1, c0:c1] * prev2 + taps_ref[1:2, c0:c1] * prev1
            + taps_ref[2:3, c0:c1] * cur)


def _load_as_bf16(src_hbm, dst_ref, stage, sems):
    rows, width = src_hbm.shape
    chunk = stage.shape[1]
    assert rows % chunk == 0 and width <= stage.shape[2]
    n = rows // chunk
    ahead = STAGE_SLOTS - 1

    def copy(i, slot):
        r0 = i * chunk if isinstance(i, int) else pl.multiple_of(i * chunk, chunk)
        return pltpu.make_async_copy(src_hbm.at[pl.ds(r0, chunk), :],
                                     stage.at[slot, :, 0:width], sems.at[slot])

    for i in range(min(ahead, n)):
        copy(i, i).start()

    def body(i, carry):
        slot = lax.rem(i, STAGE_SLOTS)
        copy(i, slot).wait()

        @pl.when(i + ahead < n)
        def _():
            copy(i + ahead, lax.rem(i + ahead, STAGE_SLOTS)).start()

        r0 = pl.multiple_of(i * chunk, chunk)
        dst_ref[pl.ds(r0, chunk), :] = stage[slot, :, 0:width].astype(BF16)
        return carry

    lax.fori_loop(0, n, body, 0)


def _block_kernel(tiles_per_seq, final_norm,
                  sinks_ref, pos_ref, x_ref, xnext_ref, invf_ref, prm_ref, win_hbm, wup_hbm,
                  wco_hbm, wmo_hbm, wao_hbm, wdn_hbm, o_ref,
                  zbuf, kbuf, vbuf, qbuf, abuf, x1buf, u1buf, umbuf, qkvbuf, ubuf, hbuf,
                  win_ref, wup_ref, wco_ref, wmo_ref, wao_ref, wdn_ref,
                  stage_wide, stage_narrow, stage_sems):
    t = pl.program_id(0)
    tm = x_ref.shape[1]
    mixer_seq_start = lax.rem(t, tiles_per_seq) == 0
    ffn_seq_start = lax.rem(t + tiles_per_seq - 1, tiles_per_seq) == 0

    bin_ref = prm_ref.at[ROW_B_IN:ROW_B_IN + 1, :]
    nmw_ref = prm_ref.at[ROW_NORM_MIX:ROW_NORM_MIX + 1, 0:D_MODEL]
    bao_ref = prm_ref.at[ROW_B_ATTN_OUT:ROW_B_ATTN_OUT + 1, 0:D_MODEL]
    nfw_ref = prm_ref.at[ROW_NORM_FFN:ROW_NORM_FFN + 1, 0:D_MODEL]
    fin_ref = prm_ref.at[ROW_NORM_FINAL:ROW_NORM_FINAL + 1, 0:D_MODEL]
    cmw_ref = prm_ref.at[ROW_CONV_MIX:ROW_CONV_MIX + 3, 0:D_MODEL]
    fcw_ref = prm_ref.at[ROW_FFN_CONV:ROW_FFN_CONV + 3, 0:D_FF]
    fcb_ref = prm_ref.at[ROW_FFN_CONV_B:ROW_FFN_CONV_B + 1, 0:D_FF]

    @pl.when(t == 0)
    def _():
        x1buf[...] = jnp.zeros(x1buf.shape, F32)
        u1buf[...] = jnp.zeros(u1buf.shape, BF16)
        _load_as_bf16(win_hbm, win_ref, stage_wide, stage_sems)
        _load_as_bf16(wup_hbm, wup_ref, stage_wide, stage_sems)
        _load_as_bf16(wco_hbm, wco_ref, stage_narrow, stage_sems)
        _load_as_bf16(wmo_hbm, wmo_ref, stage_narrow, stage_sems)
        _load_as_bf16(wao_hbm, wao_ref, stage_narrow, stage_sems)
        _load_as_bf16(wdn_hbm, wdn_ref, stage_narrow, stage_sems)
        u_first = _rmsnorm(x_ref[0], nmw_ref[...]).astype(BF16)
        umbuf[...] = u_first
        qkvbuf[...] = (_dot(u_first, win_ref[:, OFF_Q:OFF_GC]) + bin_ref[:, OFF_Q:OFF_GC])

    @pl.when(mixer_seq_start)
    def _():
        zbuf[0:HALO, :] = jnp.zeros((HALO, D_MODEL), F32)
        kbuf[:, 0:BLOCK, :] = jnp.zeros((4, BLOCK, LANES), BF16)
        vbuf[:, 0:BLOCK] = jnp.zeros((KV_WIDTH, BLOCK), BF16)

    @pl.when(jnp.logical_or(ffn_seq_start, t == 0))
    def _():
        ubuf[0:HALO, :] = jnp.zeros((HALO, D_FF), F32)

    x_prev = x1buf[...]
    u_prev = u1buf[...]
    x = x_ref[0]
    u = umbuf[...]

    def proj_of(lhs, off, width):
        return _dot(lhs, win_ref[:, off:off + width]) + bin_ref[:, off:off + width]

    def proj(off, width):
        return proj_of(u, off, width)

    def ffn_up(c0, c1):
        up = _dot(u_prev, wup_ref[:, c0:c1])
        a = _causal_conv3(ubuf, up, fcw_ref, c0, c1) + fcb_ref[:, c0:c1]
        gate = _dot(u_prev, wup_ref[:, D_FF + c0:D_FF + c1])
        hbuf[:, c0:c1] = (jax.nn.silu(a) * gate).astype(BF16)

    qkv = qkvbuf[...]
    ffn_up(*FF_CHUNKS[0])

    ang = invf_ref[...] * pos_ref[0].astype(F32)
    cos_t, sin_t = jnp.cos(ang), jnp.sin(ang)
    cos = jnp.concatenate([cos_t] * 4, axis=0).T
    sin = jnp.concatenate([-sin_t, sin_t, -sin_t, sin_t], axis=0).T

    lane = lax.broadcasted_iota(jnp.int32, (tm, LANES), 1)
    first_half = (lane & (HEAD_DIM // 2)) == 0
    low_head = lane < HEAD_DIM

    def rope(v, c, s):
        partner = jnp.where(first_half, pltpu.roll(v, LANES - HEAD_DIM // 2, 1),
                            pltpu.roll(v, HEAD_DIM // 2, 1))
        return v * c + partner * s

    scale = HEAD_DIM ** -0.5
    cos_q, sin_q = cos * scale, sin * scale
    for c in range(Q_WIDTH // LANES):
        sl = slice(c * LANES, (c + 1) * LANES)
        qbuf[:, sl] = rope(qkv[:, sl], cos_q, sin_q).astype(BF16)

    k = rope(qkv[:, Q_WIDTH:Q_WIDTH + KV_WIDTH], cos, sin)
    krot = pltpu.roll(k, HEAD_DIM, 1)
    kzero = jnp.zeros_like(k)
    rows = slice(BLOCK, BLOCK + tm)
    kbuf[0, rows, :] = jnp.where(low_head, k, kzero).astype(BF16)
    kbuf[1, rows, :] = jnp.where(low_head, kzero, krot).astype(BF16)
    kbuf[2, rows, :] = jnp.where(low_head, krot, kzero).astype(BF16)
    kbuf[3, rows, :] = jnp.where(low_head, kzero, k).astype(BF16)
    vbuf[:, BLOCK:BLOCK + tm] = qkv[:, Q_WIDTH + KV_WIDTH:].T.astype(BF16)

    key2 = lax.broadcasted_iota(jnp.int32, (2 * BLOCK, 2 * BLOCK), 0)
    qry2 = lax.broadcasted_iota(jnp.int32, (2 * BLOCK, 2 * BLOCK), 1) & (BLOCK - 1)
    band = (key2 > qry2) & (key2 <= qry2 + BLOCK)
    neg_inf = jnp.full((2 * BLOCK, 2 * BLOCK), -jnp.inf, F32)
    zeros2 = jnp.zeros((2 * BLOCK, 2 * BLOCK), F32)
    bias = jnp.where(band, zeros2, neg_inf)
    bias_seq_start = jnp.where(band & (key2 >= BLOCK), zeros2, neg_inf)
    bias_first = jnp.where(mixer_seq_start, bias_seq_start, bias)

    blocks =[(i, g) for i in range(tm // BLOCK) for g in range(N_KV_HEADS)]
    scores = {}
    for i, g in blocks:
        qrows = slice(i * BLOCK, (i + 1) * BLOCK)
        krows = slice(i * BLOCK, (i + 2) * BLOCK)
        blk_bias = bias_first if i == 0 else bias
        c0 = g * 2 * LANES
        qpair = jnp.concatenate([qbuf[qrows, c0:c0 + LANES],
                                 qbuf[qrows, c0 + LANES:c0 + 2 * LANES]], axis=0)
        scores[i, g] = (_dot_nt(kbuf[2 * g, krows, :], qpair) + blk_bias,
                        _dot_nt(kbuf[2 * g + 1, krows, :], qpair) + blk_bias)

    z = proj(OFF_CC, D_MODEL) * proj(OFF_CX, D_MODEL)

    top_pair = lax.broadcasted_iota(jnp.int32, (1, 2 * BLOCK), 1) < BLOCK
    sum_row = lax.broadcasted_iota(jnp.int32, (2 * HALO, 4 * BLOCK), 0)
    sum_col = lax.broadcasted_iota(jnp.int32, (2 * HALO, 4 * BLOCK), 1)
    sum_rows = jnp.where(sum_row == (sum_col >= 2 * BLOCK).astype(jnp.int32), 1.0, 0.0)
    sum_rows = sum_rows.astype(BF16)
    vzero = jnp.zeros((HEAD_DIM, 2 * BLOCK), BF16)

    for i, g in blocks:
        qrows = slice(i * BLOCK, (i + 1) * BLOCK)
        kcols = slice(i * BLOCK, (i + 2) * BLOCK)
        c0 = g * 2 * LANES
        s_even, s_odd = scores[i, g]
        sink_even = jnp.where(top_pair, sinks_ref[4 * g], sinks_ref[4 * g + 2])
        sink_odd = jnp.where(top_pair, sinks_ref[4 * g + 1], sinks_ref[4 * g + 3])
        m_even = jnp.maximum(jnp.max(s_even, axis=0, keepdims=True), sink_even)
        m_odd = jnp.maximum(jnp.max(s_odd, axis=0, keepdims=True), sink_odd)
        p_t = jnp.concatenate([jnp.exp(s_even - m_even).astype(BF16),
                               jnp.exp(s_odd - m_odd).astype(BF16)], axis=0)
        v_t = vbuf[g * HEAD_DIM:(g + 1) * HEAD_DIM, kcols]
        lhs = jnp.concatenate([jnp.concatenate([v_t, vzero], axis=1),
                               jnp.concatenate([vzero, v_t], axis=1), sum_rows], axis=0)
        r = _dot(lhs, p_t)
        inv_even = 1.0 / (r[2 * HEAD_DIM:2 * HEAD_DIM + 1] + jnp.exp(sink_even - m_even))
        inv_odd = 1.0 / (r[2 * HEAD_DIM + 1:2 * HEAD_DIM + 2] + jnp.exp(sink_odd - m_odd))
        out = jnp.concatenate([r[0:HEAD_DIM] * inv_even,
                               r[HEAD_DIM:2 * HEAD_DIM] * inv_odd], axis=0).T.astype(BF16)
        abuf[qrows, c0:c0 + LANES] = out[:BLOCK]
        abuf[qrows, c0 + LANES:c0 + 2 * LANES] = out[BLOCK:]

    kbuf[:, 0:BLOCK, :] = kbuf[:, tm:tm + BLOCK, :]
    vbuf[:, 0:BLOCK] = vbuf[:, tm:tm + BLOCK]

    ffn_up(*FF_CHUNKS[1])

    conv = _causal_conv3(zbuf, z, cmw_ref, 0, D_MODEL)
    gated = (proj(OFF_CB, D_MODEL) * conv).astype(BF16)
    ffn_up(*FF_CHUNKS[2])
    gate_conv = proj(OFF_GC, D_MODEL)
    gate_attn = proj(OFF_GA, D_MODEL)
    y_conv = _dot(gated, wco_ref[...])

    y_attn = _dot(abuf[...], wao_ref[...]) + bao_ref[...]

    y = x_prev + _dot(hbuf[...], wdn_ref[...])
    o_ref[0] = _rmsnorm(y, fin_ref[...]) if final_norm else y

    merged = jax.nn.sigmoid(gate_conv) * y_conv + jax.nn.sigmoid(gate_attn) * y_attn
    x1 = x + _dot(merged.astype(BF16), wmo_ref[...])
    x1buf[...] = x1
    u1buf[...] = _rmsnorm(x1, nfw_ref[...]).astype(BF16)

    u_next = _rmsnorm(xnext_ref[0], nmw_ref[...]).astype(BF16)
    umbuf[...] = u_next
    qkvbuf[...] = proj_of(u_next, OFF_Q, Q_WIDTH + 2 * KV_WIDTH)


def _whole(space=pltpu.VMEM):
    return pl.BlockSpec(memory_space=space)


def kernel(x, positions, norm_mix_w, w_in, b_in, conv_mix_w, w_conv_out, w_attn_out, b_attn_out,
           sinks, w_mix_out, norm_ffn_w, w_ffn_up, ffn_conv_w, ffn_conv_b, w_ffn_down, norm_final_w):
    b, s, d = x.shape
    depth = w_in.shape[0]
    tm = TOKEN_TILE
    assert d == D_MODEL and s % tm == 0 and tm % BLOCK == 0
    nj = s // tm
    n_tiles = b * nj

    def in_tile(t):
        return jnp.minimum(t, n_tiles - 1)

    x_spec = pl.BlockSpec((1, tm, D_MODEL), lambda t: (in_tile(t), 0, 0))
    xnext_spec = pl.BlockSpec((1, tm, D_MODEL), lambda t: (in_tile(t + 1), 0, 0))
    pos_spec = pl.BlockSpec((1, 1, tm), lambda t: (in_tile(t), 0, 0))
    out_spec = pl.BlockSpec((1, tm, D_MODEL), lambda t: (jnp.maximum(t - 1, 0), 0, 0))

    inv_freq = (ROPE_THETA ** (-jnp.arange(0, HEAD_DIM, 2, dtype=F32) / HEAD_DIM)).reshape(-1, 1)
    pos_tiles = positions.reshape(n_tiles, 1, tm)

    def rows(v):
        v = v.reshape(-1, v.shape[-1])
        return jnp.pad(v, ((0, 0), (0, IN_WIDTH - v.shape[-1])))

    def layer(final_norm):
        return pl.pallas_call(
            functools.partial(_block_kernel, nj, final_norm),
            grid=(n_tiles + 1,),
            in_specs=([_whole(pltpu.SMEM), pos_spec, x_spec, xnext_spec] + [_whole()] * 2
                      + [pl.BlockSpec(memory_space=pl.ANY)] * 6),
            out_specs=out_spec,
            out_shape=jax.ShapeDtypeStruct((n_tiles, tm, D_MODEL), F32),
            scratch_shapes=[
                pltpu.VMEM((HALO, D_MODEL), F32),
                pltpu.VMEM((4, BLOCK + tm, LANES), BF16),
                pltpu.VMEM((KV_WIDTH, BLOCK + tm), BF16),
                pltpu.VMEM((tm, Q_WIDTH), BF16),
                pltpu.VMEM((tm, Q_WIDTH), BF16),
                pltpu.VMEM((tm, D_MODEL), F32),
                pltpu.VMEM((tm, D_MODEL), BF16),
                pltpu.VMEM((tm, D_MODEL), BF16),
                pltpu.VMEM((tm, Q_WIDTH + 2 * KV_WIDTH), F32),
                pltpu.VMEM((HALO, D_FF), F32),
                pltpu.VMEM((tm, D_FF), BF16),
                pltpu.VMEM((D_MODEL, IN_WIDTH), BF16),
                pltpu.VMEM((D_MODEL, 2 * D_FF), BF16),
                pltpu.VMEM((D_MODEL, D_MODEL), BF16),
                pltpu.VMEM((D_MODEL, D_MODEL), BF16),
                pltpu.VMEM((Q_WIDTH, D_MODEL), BF16),
                pltpu.VMEM((D_FF, D_MODEL), BF16),
                pltpu.VMEM((STAGE_SLOTS, STAGE_ROWS_WIDE, IN_WIDTH), F32),
                pltpu.VMEM((STAGE_SLOTS, STAGE_ROWS_NARROW, D_MODEL), F32),
                pltpu.SemaphoreType.DMA((STAGE_SLOTS,)),
            ],
            compiler_params=pltpu.CompilerParams(dimension_semantics=("arbitrary",),
                                                 vmem_limit_bytes=VMEM_LIMIT_BYTES),
            name="decoder_block",
        )

    x = x.reshape(n_tiles, tm, D_MODEL)
    for l in range(depth):
        params = jnp.concatenate(
            [rows(b_in[l]), rows(norm_mix_w[l]), rows(b_attn_out[l]), rows(norm_ffn_w[l]),
             rows(norm_final_w), rows(conv_mix_w[l]), rows(ffn_conv_w[l]), rows(ffn_conv_b[l])],
            axis=0)
        x = layer(l == depth - 1)(
            sinks[l], pos_tiles, x, x, inv_freq, params, w_in[l], w_ffn_up[l], w_conv_out[l],
            w_mix_out[l], w_attn_out[l], w_ffn_down[l])
    return x.reshape(b, s, D_MODEL)
```

```python
import functools

import jax
import jax.numpy as jnp
from jax import lax
from jax.experimental import pallas as pl
from jax.experimental.pallas import tpu as pltpu

D_MODEL = 1024
HEAD_DIM = 64
N_Q_HEADS = 8
N_KV_HEADS = 2
BLOCK = 128
ROPE_THETA = 10000.0
Q_WIDTH = N_Q_HEADS * HEAD_DIM
KV_WIDTH = N_KV_HEADS * HEAD_DIM
D_FF = 2816
EPS = 1e-5

OFF_CB = 0
OFF_CC = OFF_CB + D_MODEL
OFF_CX = OFF_CC + D_MODEL
OFF_Q = OFF_CX + D_MODEL
OFF_K = OFF_Q + Q_WIDTH
OFF_V = OFF_K + KV_WIDTH
OFF_GC = OFF_V + KV_WIDTH
OFF_GA = OFF_GC + D_MODEL
IN_WIDTH = OFF_GA + D_MODEL

ROW_B_IN, ROW_NORM_MIX, ROW_B_ATTN_OUT, ROW_NORM_FFN, ROW_NORM_FINAL = 0, 1, 2, 3, 4
ROW_CONV_MIX, ROW_FFN_CONV, ROW_FFN_CONV_B = 5, 8, 11

LANES = 128
HALO = 8
TOKEN_TILE = 256
FF_CHUNKS = ((0, 1024), (1024, 2048), (2048, D_FF))
VMEM_LIMIT_BYTES = 60 * 1024 * 1024

STAGE_SLOTS = 4
STAGE_ROWS_WIDE = 32
STAGE_ROWS_NARROW = 128

F32 = jnp.float32
BF16 = jnp.bfloat16


def _rmsnorm(x, w):
    ms = jnp.mean(x * x, axis=-1, keepdims=True)
    return x * lax.rsqrt(ms + EPS) * w


def _dot(a, b):
    return jnp.dot(a, b, preferred_element_type=F32)


def _dot_nt(a, b):
    return lax.dot_general(a, b, (((1,), (1,)), ((), ())), preferred_element_type=F32)


def _causal_conv3(halo_ref, cur, taps_ref, c0, c1):
    tm = cur.shape[0]
    first_row = lax.broadcasted_iota(jnp.int32, (HALO, c1 - c0), 0) == 0

    def shift_down(v, halo_row):
        rolled = pltpu.roll(v, 1, 0)
        head = jnp.where(first_row, halo_row, rolled[0:HALO])
        return jnp.concatenate([head, rolled[HALO:]], axis=0)

    prev1 = shift_down(cur, halo_ref[HALO - 1:HALO, c0:c1])
    prev2 = shift_down(prev1, halo_ref[HALO - 2:HALO - 1, c0:c1])
    halo_ref[HALO - 2:HALO, c0:c1] = cur[tm - 2:tm]
    return (taps_ref[0:1, c0:c1] * prev2 + taps_ref[1:2, c0:c1] * prev1
            + taps_ref[2:3, c0:c1] * cur)


def _load_as_bf16(src_hbm, dst_ref, stage, sems):
    rows, width = src_hbm.shape
    chunk = stage.shape[1]
    assert rows % chunk == 0 and width <= stage.shape[2]
    n = rows // chunk
    ahead = STAGE_SLOTS - 1

    def copy(i, slot):
        r0 = i * chunk if isinstance(i, int) else pl.multiple_of(i * chunk, chunk)
        return pltpu.make_async_copy(src_hbm.at[pl.ds(r0, chunk), :],
                                     stage.at[slot, :, 0:width], sems.at[slot])

    for i in range(min(ahead, n)):
        copy(i, i).start()

    def body(i, carry):
        slot = lax.rem(i, STAGE_SLOTS)
        copy(i, slot).wait()

        @pl.when(i + ahead < n)
        def _():
            copy(i + ahead, lax.rem(i + ahead, STAGE_SLOTS)).start()

        r0 = pl.multiple_of(i * chunk, chunk)
        dst_ref[pl.ds(r0, chunk), :] = stage[slot, :, 0:width].astype(BF16)
        return carry

    lax.fori_loop(0, n, body, 0)


def _block_kernel(tiles_per_seq, final_norm,
                  sinks_ref, pos_ref, x_ref, xnext_ref, invf_ref, prm_ref, win_hbm, wup_hbm,
                  wco_hbm, wmo_hbm, wao_hbm, wdn_hbm, o_ref,
                  zbuf, kbuf, vbuf, qbuf, abuf, x1buf, u1buf, umbuf, qkvbuf, ubuf, hbuf,
                  win_ref, wup_ref, wco_ref, wmo_ref, wao_ref, wdn_ref,
                  stage_wide, stage_narrow, stage_sems):
    t = pl.program_id(0)
    tm = x_ref.shape[1]
    mixer_seq_start = lax.rem(t, tiles_per_seq) == 0
    ffn_seq_start = lax.rem(t + tiles_per_seq - 1, tiles_per_seq) == 0

    bin_ref = prm_ref.at[ROW_B_IN:ROW_B_IN + 1, :]
    nmw_ref = prm_ref.at[ROW_NORM_MIX:ROW_NORM_MIX + 1, 0:D_MODEL]
    bao_ref = prm_ref.at[ROW_B_ATTN_OUT:ROW_B_ATTN_OUT + 1, 0:D_MODEL]
    nfw_ref = prm_ref.at[ROW_NORM_FFN:ROW_NORM_FFN + 1, 0:D_MODEL]
    fin_ref = prm_ref.at[ROW_NORM_FINAL:ROW_NORM_FINAL + 1, 0:D_MODEL]
    cmw_ref = prm_ref.at[ROW_CONV_MIX:ROW_CONV_MIX + 3, 0:D_MODEL]
    fcw_ref = prm_ref.at[ROW_FFN_CONV:ROW_FFN_CONV + 3, 0:D_FF]
    fcb_ref = prm_ref.at[ROW_FFN_CONV_B:ROW_FFN_CONV_B + 1, 0:D_FF]

    @pl.when(t == 0)
    def _():
        x1buf[...] = jnp.zeros(x1buf.shape, F32)
        u1buf[...] = jnp.zeros(u1buf.shape, BF16)
        _load_as_bf16(win_hbm, win_ref, stage_wide, stage_sems)
        _load_as_bf16(wup_hbm, wup_ref, stage_wide, stage_sems)
        _load_as_bf16(wco_hbm, wco_ref, stage_narrow, stage_sems)
        _load_as_bf16(wmo_hbm, wmo_ref, stage_narrow, stage_sems)
        _load_as_bf16(wao_hbm, wao_ref, stage_narrow, stage_sems)
        _load_as_bf16(wdn_hbm, wdn_ref, stage_narrow, stage_sems)
        u_first = _rmsnorm(x_ref[0], nmw_ref[...]).astype(BF16)
        umbuf[...] = u_first
        qkvbuf[...] = (_dot(u_first, win_ref[:, OFF_Q:OFF_GC]) + bin_ref[:, OFF_Q:OFF_GC])

    @pl.when(mixer_seq_start)
    def _():
        zbuf[0:HALO, :] = jnp.zeros((HALO, D_MODEL), F32)
        kbuf[:, 0:BLOCK, :] = jnp.zeros((4, BLOCK, LANES), BF16)
        vbuf[:, 0:BLOCK] = jnp.zeros((KV_WIDTH, BLOCK), BF16)

    @pl.when(jnp.logical_or(ffn_seq_start, t == 0))
    def _():
        ubuf[0:HALO, :] = jnp.zeros((HALO, D_FF), F32)

    x_prev = x1buf[...]
    u_prev = u1buf[...]
    x = x_ref[0]
    u = umbuf[...]

    def proj_of(lhs, off, width):
        return _dot(lhs, win_ref[:, off:off + width]) + bin_ref[:, off:off + width]

    def proj(off, width):
        return proj_of(u, off, width)

    def ffn_up(c0, c1):
        up = _dot(u_prev, wup_ref[:, c0:c1])
        a = _causal_conv3(ubuf, up, fcw_ref, c0, c1) + fcb_ref[:, c0:c1]
        gate = _dot(u_prev, wup_ref[:, D_FF + c0:D_FF + c1])
        hbuf[:, c0:c1] = (jax.nn.silu(a) * gate).astype(BF16)

    qkv = qkvbuf[...]
    ffn_up(*FF_CHUNKS[0])
    ffn_up(*FF_CHUNKS[1])

    ang = invf_ref[...] * pos_ref[0].astype(F32)
    cos_t, sin_t = jnp.cos(ang), jnp.sin(ang)
    cos = jnp.concatenate([cos_t] * 4, axis=0).T
    sin = jnp.concatenate([-sin_t, sin_t, -sin_t, sin_t], axis=0).T

    lane = lax.broadcasted_iota(jnp.int32, (tm, LANES), 1)
    first_half = (lane & (HEAD_DIM // 2)) == 0
    low_head = lane < HEAD_DIM

    def rope(v, c, s):
        partner = jnp.where(first_half, pltpu.roll(v, LANES - HEAD_DIM // 2, 1),
                            pltpu.roll(v, HEAD_DIM // 2, 1))
        return v * c + partner * s

    scale = HEAD_DIM ** -0.5
    cos_q, sin_q = cos * scale, sin * scale
    for c in range(Q_WIDTH // LANES):
        sl = slice(c * LANES, (c + 1) * LANES)
        qbuf[:, sl] = rope(qkv[:, sl], cos_q, sin_q).astype(BF16)

    k = rope(qkv[:, Q_WIDTH:Q_WIDTH + KV_WIDTH], cos, sin)
    krot = pltpu.roll(k, HEAD_DIM, 1)
    kzero = jnp.zeros_like(k)
    rows = slice(BLOCK, BLOCK + tm)
    kbuf[0, rows, :] = jnp.where(low_head, k, kzero).astype(BF16)
    kbuf[1, rows, :] = jnp.where(low_head, kzero, krot).astype(BF16)
    kbuf[2, rows, :] = jnp.where(low_head, krot, kzero).astype(BF16)
    kbuf[3, rows, :] = jnp.where(low_head, kzero, k).astype(BF16)
    vbuf[:, BLOCK:BLOCK + tm] = qkv[:, Q_WIDTH + KV_WIDTH:].T.astype(BF16)

    key2 = lax.broadcasted_iota(jnp.int32, (2 * BLOCK, 2 * BLOCK), 0)
    qry2 = lax.broadcasted_iota(jnp.int32, (2 * BLOCK, 2 * BLOCK), 1) & (BLOCK - 1)
    band = (key2 > qry2) & (key2 <= qry2 + BLOCK)
    neg_inf = jnp.full((2 * BLOCK, 2 * BLOCK), -jnp.inf, F32)
    zeros2 = jnp.zeros((2 * BLOCK, 2 * BLOCK), F32)
    bias = jnp.where(band, zeros2, neg_inf)
    bias_seq_start = jnp.where(band & (key2 >= BLOCK), zeros2, neg_inf)
    bias_first = jnp.where(mixer_seq_start, bias_seq_start, bias)

    z = proj(OFF_CC, D_MODEL) * proj(OFF_CX, D_MODEL)

    blocks = [(i, g) for i in range(tm // BLOCK) for g in range(N_KV_HEADS)]
    scores = {}
    for i, g in blocks:
        qrows = slice(i * BLOCK, (i + 1) * BLOCK)
        krows = slice(i * BLOCK, (i + 2) * BLOCK)
        blk_bias = bias_first if i == 0 else bias
        c0 = g * 2 * LANES
        qpair = jnp.concatenate([qbuf[qrows, c0:c0 + LANES],
                                 qbuf[qrows, c0 + LANES:c0 + 2 * LANES]], axis=0)
        scores[i, g] = (_dot_nt(kbuf[2 * g, krows, :], qpair) + blk_bias,
                        _dot_nt(kbuf[2 * g + 1, krows, :], qpair) + blk_bias)

    ffn_up(*FF_CHUNKS[2])

    top_pair = lax.broadcasted_iota(jnp.int32, (1, 2 * BLOCK), 1) < BLOCK
    sum_row = lax.broadcasted_iota(jnp.int32, (2 * HALO, 4 * BLOCK), 0)
    sum_col = lax.broadcasted_iota(jnp.int32, (2 * HALO, 4 * BLOCK), 1)
    sum_rows = jnp.where(sum_row == (sum_col >= 2 * BLOCK).astype(jnp.int32), 1.0, 0.0)
    sum_rows = sum_rows.astype(BF16)
    vzero = jnp.zeros((HEAD_DIM, 2 * BLOCK), BF16)

    for i, g in blocks:
        qrows = slice(i * BLOCK, (i + 1) * BLOCK)
        kcols = slice(i * BLOCK, (i + 2) * BLOCK)
        c0 = g * 2 * LANES
        s_even, s_odd = scores[i, g]
        sink_even = jnp.where(top_pair, sinks_ref[4 * g], sinks_ref[4 * g + 2])
        sink_odd = jnp.where(top_pair, sinks_ref[4 * g + 1], sinks_ref[4 * g + 3])
        m_even = jnp.maximum(jnp.max(s_even, axis=0, keepdims=True), sink_even)
        m_odd = jnp.maximum(jnp.max(s_odd, axis=0, keepdims=True), sink_odd)
        p_t = jnp.concatenate([jnp.exp(s_even - m_even).astype(BF16),
                               jnp.exp(s_odd - m_odd).astype(BF16)], axis=0)
        v_t = vbuf[g * HEAD_DIM:(g + 1) * HEAD_DIM, kcols]
        lhs = jnp.concatenate([jnp.concatenate([v_t, vzero], axis=1),
                               jnp.concatenate([vzero, v_t], axis=1), sum_rows], axis=0)
        r = _dot(lhs, p_t)
        inv_even = 1.0 / (r[2 * HEAD_DIM:2 * HEAD_DIM + 1] + jnp.exp(sink_even - m_even))
        inv_odd = 1.0 / (r[2 * HEAD_DIM + 1:2 * HEAD_DIM + 2] + jnp.exp(sink_odd - m_odd))
        out = jnp.concatenate([r[0:HEAD_DIM] * inv_even,
                               r[HEAD_DIM:2 * HEAD_DIM] * inv_odd], axis=0).T.astype(BF16)
        abuf[qrows, c0:c0 + LANES] = out[:BLOCK]
        abuf[qrows, c0 + LANES:c0 + 2 * LANES] = out[BLOCK:]

    kbuf[:, 0:BLOCK, :] = kbuf[:, tm:tm + BLOCK, :]
    vbuf[:, 0:BLOCK] = vbuf[:, tm:tm + BLOCK]

    conv = _causal_conv3(zbuf, z, cmw_ref, 0, D_MODEL)
    gated = (proj(OFF_CB, D_MODEL) * conv).astype(BF16)
    gate_conv = proj(OFF_GC, D_MODEL)
    gate_attn = proj(OFF_GA, D_MODEL)
    y_conv = _dot(gated, wco_ref[...])

    y_attn = _dot(abuf[...], wao_ref[...]) + bao_ref[...]

    y = x_prev + _dot(hbuf[...], wdn_ref[...])
    o_ref[0] = _rmsnorm(y, fin_ref[...]) if final_norm else y

    merged = jax.nn.sigmoid(gate_conv) * y_conv + jax.nn.sigmoid(gate_attn) * y_attn
    x1 = x + _dot(merged.astype(BF16), wmo_ref[...])
    x1buf[...] = x1
    u1buf[...] = _rmsnorm(x1, nfw_ref[...]).astype(BF16)

    u_next = _rmsnorm(xnext_ref[0], nmw_ref[...]).astype(BF16)
    umbuf[...] = u_next
    qkvbuf[...] = proj_of(u_next, OFF_Q, Q_WIDTH + 2 * KV_WIDTH)


def _whole(space=pltpu.VMEM):
    return pl.BlockSpec(memory_space=space)


def kernel(x, positions, norm_mix_w, w_in, b_in, conv_mix_w, w_conv_out, w_attn_out, b_attn_out,
           sinks, w_mix_out, norm_ffn_w, w_ffn_up, ffn_conv_w, ffn_conv_b, w_ffn_down, norm_final_w):
    b, s, d = x.shape
    depth = w_in.shape[0]
    tm = TOKEN_TILE
    assert d == D_MODEL and s % tm == 0 and tm % BLOCK == 0
    nj = s // tm
    n_tiles = b * nj

    def in_tile(t):
        return jnp.minimum(t, n_tiles - 1)

    x_spec = pl.BlockSpec((1, tm, D_MODEL), lambda t: (in_tile(t), 0, 0))
    xnext_spec = pl.BlockSpec((1, tm, D_MODEL), lambda t: (in_tile(t + 1), 0, 0))
    pos_spec = pl.BlockSpec((1, 1, tm), lambda t: (in_tile(t), 0, 0))
    out_spec = pl.BlockSpec((1, tm, D_MODEL), lambda t: (jnp.maximum(t - 1, 0), 0, 0))

    inv_freq = (ROPE_THETA ** (-jnp.arange(0, HEAD_DIM, 2, dtype=F32) / HEAD_DIM)).reshape(-1, 1)
    pos_tiles = positions.reshape(n_tiles, 1, tm)

    def rows(v):
        v = v.reshape(-1, v.shape[-1])
        return jnp.pad(v, ((0, 0), (0, IN_WIDTH - v.shape[-1])))

    def layer(final_norm):
        return pl.pallas_call(
            functools.partial(_block_kernel, nj, final_norm),
            grid=(n_tiles + 1,),
            in_specs=([_whole(pltpu.SMEM), pos_spec, x_spec, xnext_spec] + [_whole()] * 2
                      + [pl.BlockSpec(memory_space=pl.ANY)] * 6),
            out_specs=out_spec,
            out_shape=jax.ShapeDtypeStruct((n_tiles, tm, D_MODEL), F32),
            scratch_shapes=[
                pltpu.VMEM((HALO, D_MODEL), F32),
                pltpu.VMEM((4, BLOCK + tm, LANES), BF16),
                pltpu.VMEM((KV_WIDTH, BLOCK + tm), BF16),
                pltpu.VMEM((tm, Q_WIDTH), BF16),
                pltpu.VMEM((tm, Q_WIDTH), BF16),
                pltpu.VMEM((tm, D_MODEL), F32),
                pltpu.VMEM((tm, D_MODEL), BF16),
                pltpu.VMEM((tm, D_MODEL), BF16),
                pltpu.VMEM((tm, Q_WIDTH + 2 * KV_WIDTH), F32),
                pltpu.VMEM((2 * HALO, D_FF), F32),
                pltpu.VMEM((tm, D_FF), BF16),
                pltpu.VMEM((D_MODEL, IN_WIDTH), BF16),
                pltpu.VMEM((D_MODEL, 2 * D_FF), BF16),
                pltpu.VMEM((D_MODEL, D_MODEL), BF16),
                pltpu.VMEM((D_MODEL, D_MODEL), BF16),
                pltpu.VMEM((Q_WIDTH, D_MODEL), BF16),
                pltpu.VMEM((D_FF, D_MODEL), BF16),
                pltpu.VMEM((STAGE_SLOTS, STAGE_ROWS_WIDE, IN_WIDTH), F32),
                pltpu.VMEM((STAGE_SLOTS, STAGE_ROWS_NARROW, D_MODEL), F32),
                pltpu.SemaphoreType.DMA((STAGE_SLOTS,)),
            ],
            compiler_params=pltpu.CompilerParams(dimension_semantics=("arbitrary",),
                                                 vmem_limit_bytes=VMEM_LIMIT_BYTES),
            name="decoder_block",
        )

    x = x.reshape(n_tiles, tm, D_MODEL)
    for l in range(depth):
        params = jnp.concatenate(
            [rows(b_in[l]), rows(norm_mix_w[l]), rows(b_attn_out[l]), rows(norm_ffn_w[l]),
             rows(norm_final_w), rows(conv_mix_w[l]), rows(ffn_conv_w[l]), rows(ffn_conv_b[l])],
            axis=0)
        x = layer(l == depth - 1)(
            sinks[l], pos_tiles, x, x, inv_freq, params, w_in[l], w_ffn_up[l], w_conv_out[l],
            w_mix_out[l], w_attn_out[l], w_ffn_down[l])
    return x.reshape(b, s, D_MODEL)
```

```python
import functools

import jax
import jax.numpy as jnp
from jax import lax
from jax.experimental import pallas as pl
from jax.experimental.pallas import tpu as pltpu

D_MODEL = 1024
HEAD_DIM = 64
N_Q_HEADS = 8
N_KV_HEADS = 2
BLOCK = 128
ROPE_THETA = 10000.0
Q_WIDTH = N_Q_HEADS * HEAD_DIM
KV_WIDTH = N_KV_HEADS * HEAD_DIM
D_FF = 2816
EPS = 1e-5

OFF_CB = 0
OFF_CC = OFF_CB + D_MODEL
OFF_CX = OFF_CC + D_MODEL
OFF_Q = OFF_CX + D_MODEL
OFF_K = OFF_Q + Q_WIDTH
OFF_V = OFF_K + KV_WIDTH
OFF_GC = OFF_V + KV_WIDTH
OFF_GA = OFF_GC + D_MODEL
IN_WIDTH = OFF_GA + D_MODEL

ROW_B_IN, ROW_NORM_MIX, ROW_B_ATTN_OUT, ROW_NORM_FFN, ROW_NORM_FINAL = 0, 1, 2, 3, 4
ROW_CONV_MIX, ROW_FFN_CONV, ROW_FFN_CONV_B = 5, 8, 11

LANES = 128
HALO = 8
TOKEN_TILE = 256
FF_CHUNKS = ((0, 1024), (1024, 2048), (2048, D_FF))
VMEM_LIMIT_BYTES = 60 * 1024 * 1024

STAGE_SLOTS = 4
STAGE_ROWS_WIDE = 32
STAGE_ROWS_NARROW = 128

F32 = jnp.float32
BF16 = jnp.bfloat16


def _rmsnorm(x, w):
    ms = jnp.mean(x * x, axis=-1, keepdims=True)
    return x * lax.rsqrt(ms + EPS) * w


def _dot(a, b):
    return jnp.dot(a, b, preferred_element_type=F32)


def _dot_nt(a, b):
    return lax.dot_general(a, b, (((1,), (1,)), ((), ())), preferred_element_type=F32)


def _causal_conv3(halo_ref, cur, taps_ref, c0, c1):
    tm = cur.shape[0]
    first_row = lax.broadcasted_iota(jnp.int32, (HALO, c1 - c0), 0) == 0

    def shift_down(v, halo_row):
        rolled = pltpu.roll(v, 1, 0)
        head = jnp.where(first_row, halo_row, rolled[0:HALO])
        return jnp.concatenate([head, rolled[HALO:]], axis=0)

    prev1 = shift_down(cur, halo_ref[HALO - 1:HALO, c0:c1])
    prev2 = shift_down(prev1, halo_ref[HALO - 2:HALO - 1, c0:c1])
    halo_ref[HALO - 2:HALO, c0:c1] = cur[tm - 2:tm]
    return (taps_ref[0:1, c0:c1] * prev2 + taps_ref[1:2, c0:c1] * prev1
            + taps_ref[2:3, c0:c1] * cur)


def _paired_columns(first, second, width, dst):
    moves = []
    for j in range(width // LANES):
        moves.append((first + j * LANES, dst + 2 * j * LANES, LANES))
        moves.append((second + j * LANES, dst + (2 * j + 1) * LANES, LANES))
    return moves


def _load_as_bf16(src_hbm, dst_ref, stage, sems, column_moves=None):
    rows, width = src_hbm.shape
    chunk = stage.shape[1]
    assert rows % chunk == 0 and width <= stage.shape[2]
    n = rows // chunk
    ahead = STAGE_SLOTS - 1
    moves = column_moves or [(0, 0, width)]
    assert sum(w for _, _, w in moves) == width

    def copy(i, slot):
        r0 = i * chunk if isinstance(i, int) else pl.multiple_of(i * chunk, chunk)
        return pltpu.make_async_copy(src_hbm.at[pl.ds(r0, chunk), :],
                                     stage.at[slot, :, 0:width], sems.at[slot])

    for i in range(min(ahead, n)):
        copy(i, i).start()

    def body(i, carry):
        slot = lax.rem(i, STAGE_SLOTS)
        copy(i, slot).wait()

        @pl.when(i + ahead < n)
        def _():
            copy(i + ahead, lax.rem(i + ahead, STAGE_SLOTS)).start()

        r0 = pl.multiple_of(i * chunk, chunk)
        for src_col, dst_col, w in moves:
            dst_ref[pl.ds(r0, chunk), dst_col:dst_col + w] = (
                stage[slot, :, src_col:src_col + w].astype(BF16))
        return carry

    lax.fori_loop(0, n, body, 0)


def _block_kernel(tiles_per_seq, final_norm,
                  sinks_ref, pos_ref, x_ref, xnext_ref, invf_ref, prm_ref, win_hbm, wup_hbm,
                  wco_hbm, wmo_hbm, wao_hbm, wdn_hbm, o_ref,
                  zbuf, kbuf, vbuf, qbuf, abuf, x1buf, u1buf, umbuf, qkvbuf, ubuf, hbuf,
                  win_ref, wup_ref, wco_ref, wmo_ref, wao_ref, wdn_ref,
                  stage_wide, stage_narrow, stage_sems):
    t = pl.program_id(0)
    tm = x_ref.shape[1]
    mixer_seq_start = lax.rem(t, tiles_per_seq) == 0
    ffn_seq_start = lax.rem(t + tiles_per_seq - 1, tiles_per_seq) == 0

    bin_ref = prm_ref.at[ROW_B_IN:ROW_B_IN + 1, :]
    nmw_ref = prm_ref.at[ROW_NORM_MIX:ROW_NORM_MIX + 1, 0:D_MODEL]
    bao_ref = prm_ref.at[ROW_B_ATTN_OUT:ROW_B_ATTN_OUT + 1, 0:D_MODEL]
    nfw_ref = prm_ref.at[ROW_NORM_FFN:ROW_NORM_FFN + 1, 0:D_MODEL]
    fin_ref = prm_ref.at[ROW_NORM_FINAL:ROW_NORM_FINAL + 1, 0:D_MODEL]
    cmw_ref = prm_ref.at[ROW_CONV_MIX:ROW_CONV_MIX + 3, 0:D_MODEL]
    fcw_ref = prm_ref.at[ROW_FFN_CONV:ROW_FFN_CONV + 3, 0:D_FF]
    fcb_ref = prm_ref.at[ROW_FFN_CONV_B:ROW_FFN_CONV_B + 1, 0:D_FF]

    @pl.when(t == 0)
    def _():
        x1buf[...] = jnp.zeros(x1buf.shape, F32)
        u1buf[...] = jnp.zeros(u1buf.shape, BF16)
        _load_as_bf16(win_hbm, win_ref, stage_wide, stage_sems,
                      [(0, 0, OFF_CC)] + _paired_columns(OFF_CC, OFF_CX, D_MODEL, OFF_CC)
                      + [(OFF_Q, OFF_Q, IN_WIDTH - OFF_Q)])
        _load_as_bf16(wup_hbm, wup_ref, stage_wide, stage_sems,
                      _paired_columns(0, D_FF, D_FF, 0))
        _load_as_bf16(wco_hbm, wco_ref, stage_narrow, stage_sems)
        _load_as_bf16(wmo_hbm, wmo_ref, stage_narrow, stage_sems)
        _load_as_bf16(wao_hbm, wao_ref, stage_narrow, stage_sems)
        _load_as_bf16(wdn_hbm, wdn_ref, stage_narrow, stage_sems)
        u_first = _rmsnorm(x_ref[0], nmw_ref[...]).astype(BF16)
        umbuf[...] = u_first
        qkvbuf[...] = (_dot(u_first, win_ref[:, OFF_Q:OFF_GC]) + bin_ref[:, OFF_Q:OFF_GC])

    @pl.when(mixer_seq_start)
    def _():
        zbuf[0:HALO, :] = jnp.zeros((HALO, D_MODEL), F32)
        kbuf[:, 0:BLOCK, :] = jnp.zeros((4, BLOCK, LANES), BF16)
        vbuf[:, 0:BLOCK] = jnp.zeros((KV_WIDTH, BLOCK), BF16)

    @pl.when(jnp.logical_or(ffn_seq_start, t == 0))
    def _():
        ubuf[0:HALO, :] = jnp.zeros((HALO, D_FF), F32)

    x_prev = x1buf[...]
    u_prev = u1buf[...]
    x = x_ref[0]
    u = umbuf[...]

    def proj_of(lhs, off, width):
        return _dot(lhs, win_ref[:, off:off + width]) + bin_ref[:, off:off + width]

    def proj(off, width):
        return proj_of(u, off, width)

    def ffn_up(c0, c1):
        both = _dot(u_prev, wup_ref[:, 2 * c0:2 * c1])
        for j in range((c1 - c0) // LANES):
            up = both[:, 2 * j * LANES:(2 * j + 1) * LANES]
            gate = both[:, (2 * j + 1) * LANES:(2 * j + 2) * LANES]
            lo = c0 + j * LANES
            a = _causal_conv3(ubuf, up, fcw_ref, lo, lo + LANES) + fcb_ref[:, lo:lo + LANES]
            hbuf[:, lo:lo + LANES] = (jax.nn.silu(a) * gate).astype(BF16)

    qkv = qkvbuf[...]
    ffn_up(*FF_CHUNKS[0])
    ffn_up(*FF_CHUNKS[1])

    ang = invf_ref[...] * pos_ref[0].astype(F32)
    cos_t, sin_t = jnp.cos(ang), jnp.sin(ang)
    cos = jnp.concatenate([cos_t] * 4, axis=0).T
    sin = jnp.concatenate([-sin_t, sin_t, -sin_t, sin_t], axis=0).T

    lane = lax.broadcasted_iota(jnp.int32, (tm, LANES), 1)
    first_half = (lane & (HEAD_DIM // 2)) == 0
    low_head = lane < HEAD_DIM

    def rope(v, c, s):
        partner = jnp.where(first_half, pltpu.roll(v, LANES - HEAD_DIM // 2, 1),
                            pltpu.roll(v, HEAD_DIM // 2, 1))
        return v * c + partner * s

    scale = HEAD_DIM ** -0.5
    cos_q, sin_q = cos * scale, sin * scale
    for c in range(Q_WIDTH // LANES):
        sl = slice(c * LANES, (c + 1) * LANES)
        qbuf[:, sl] = rope(qkv[:, sl], cos_q, sin_q).astype(BF16)

    k = rope(qkv[:, Q_WIDTH:Q_WIDTH + KV_WIDTH], cos, sin)
    krot = pltpu.roll(k, HEAD_DIM, 1)
    kzero = jnp.zeros_like(k)
    rows = slice(BLOCK, BLOCK + tm)
    kbuf[0, rows, :] = jnp.where(low_head, k, kzero).astype(BF16)
    kbuf[1, rows, :] = jnp.where(low_head, kzero, krot).astype(BF16)
    kbuf[2, rows, :] = jnp.where(low_head, krot, kzero).astype(BF16)
    kbuf[3, rows, :] = jnp.where(low_head, kzero, k).astype(BF16)
    vbuf[:, BLOCK:BLOCK + tm] = qkv[:, Q_WIDTH + KV_WIDTH:].T.astype(BF16)

    key2 = lax.broadcasted_iota(jnp.int32, (2 * BLOCK, 2 * BLOCK), 0)
    qry2 = lax.broadcasted_iota(jnp.int32, (2 * BLOCK, 2 * BLOCK), 1) & (BLOCK - 1)
    band = (key2 > qry2) & (key2 <= qry2 + BLOCK)
    neg_inf = jnp.full((2 * BLOCK, 2 * BLOCK), -jnp.inf, F32)
    zeros2 = jnp.zeros((2 * BLOCK, 2 * BLOCK), F32)
    bias = jnp.where(band, zeros2, neg_inf)
    bias_seq_start = jnp.where(band & (key2 >= BLOCK), zeros2, neg_inf)
    bias_first = jnp.where(mixer_seq_start, bias_seq_start, bias)

    ccx = _dot(u, win_ref[:, OFF_CC:OFF_Q])
    z = jnp.concatenate(
        [(ccx[:, 2 * j * LANES:(2 * j + 1) * LANES]
          + bin_ref[:, OFF_CC + j * LANES:OFF_CC + (j + 1) * LANES])
         * (ccx[:, (2 * j + 1) * LANES:(2 * j + 2) * LANES]
            + bin_ref[:, OFF_CX + j * LANES:OFF_CX + (j + 1) * LANES])
         for j in range(D_MODEL // LANES)], axis=1)

    blocks = [(i, g) for i in range(tm // BLOCK) for g in range(N_KV_HEADS)]
    scores = {}
    for i, g in blocks:
        qrows = slice(i * BLOCK, (i + 1) * BLOCK)
        krows = slice(i * BLOCK, (i + 2) * BLOCK)
        blk_bias = bias_first if i == 0 else bias
        c0 = g * 2 * LANES
        qpair = jnp.concatenate([qbuf[qrows, c0:c0 + LANES],
                                 qbuf[qrows, c0 + LANES:c0 + 2 * LANES]], axis=0)
        scores[i, g] = (_dot_nt(kbuf[2 * g, krows, :], qpair) + blk_bias,
                        _dot_nt(kbuf[2 * g + 1, krows, :], qpair) + blk_bias)

    ffn_up(*FF_CHUNKS[2])

    top_pair = lax.broadcasted_iota(jnp.int32, (1, 2 * BLOCK), 1) < BLOCK
    sum_row = lax.broadcasted_iota(jnp.int32, (2 * HALO, 4 * BLOCK), 0)
    sum_col = lax.broadcasted_iota(jnp.int32, (2 * HALO, 4 * BLOCK), 1)
    sum_rows = jnp.where(sum_row == (sum_col >= 2 * BLOCK).astype(jnp.int32), 1.0, 0.0)
    sum_rows = sum_rows.astype(BF16)
    vzero = jnp.zeros((HEAD_DIM, 2 * BLOCK), BF16)

    for i, g in blocks:
        qrows = slice(i * BLOCK, (i + 1) * BLOCK)
        kcols = slice(i * BLOCK, (i + 2) * BLOCK)
        c0 = g * 2 * LANES
        s_even, s_odd = scores[i, g]
        sink_even = jnp.where(top_pair, sinks_ref[4 * g], sinks_ref[4 * g + 2])
        sink_odd = jnp.where(top_pair, sinks_ref[4 * g + 1], sinks_ref[4 * g + 3])
        m_even = jnp.maximum(jnp.max(s_even, axis=0, keepdims=True), sink_even)
        m_odd = jnp.maximum(jnp.max(s_odd, axis=0, keepdims=True), sink_odd)
        p_t = jnp.concatenate([jnp.exp(s_even - m_even).astype(BF16),
                               jnp.exp(s_odd - m_odd).astype(BF16)], axis=0)
        v_t = vbuf[g * HEAD_DIM:(g + 1) * HEAD_DIM, kcols]
        lhs = jnp.concatenate([jnp.concatenate([v_t, vzero], axis=1),
                               jnp.concatenate([vzero, v_t], axis=1), sum_rows], axis=0)
        r = _dot(lhs, p_t)
        inv_even = 1.0 / (r[2 * HEAD_DIM:2 * HEAD_DIM + 1] + jnp.exp(sink_even - m_even))
        inv_odd = 1.0 / (r[2 * HEAD_DIM + 1:2 * HEAD_DIM + 2] + jnp.exp(sink_odd - m_odd))
        out = jnp.concatenate([r[0:HEAD_DIM] * inv_even,
                               r[HEAD_DIM:2 * HEAD_DIM] * inv_odd], axis=0).T.astype(BF16)
        abuf[qrows, c0:c0 + LANES] = out[:BLOCK]
        abuf[qrows, c0 + LANES:c0 + 2 * LANES] = out[BLOCK:]

    kbuf[:, 0:BLOCK, :] = kbuf[:, tm:tm + BLOCK, :]
    vbuf[:, 0:BLOCK] = vbuf[:, tm:tm + BLOCK]

    conv = _causal_conv3(zbuf, z, cmw_ref, 0, D_MODEL)
    gated = (proj(OFF_CB, D_MODEL) * conv).astype(BF16)
    gate_conv = proj(OFF_GC, D_MODEL)
    gate_attn = proj(OFF_GA, D_MODEL)
    y_conv = _dot(gated, wco_ref[...])

    y_attn = _dot(abuf[...], wao_ref[...]) + bao_ref[...]

    y = x_prev + _dot(hbuf[...], wdn_ref[...])
    o_ref[0] = _rmsnorm(y, fin_ref[...]) if final_norm else y

    merged = jax.nn.sigmoid(gate_conv) * y_conv + jax.nn.sigmoid(gate_attn) * y_attn
    x1 = x + _dot(merged.astype(BF16), wmo_ref[...])
    x1buf[...] = x1
    u1buf[...] = _rmsnorm(x1, nfw_ref[...]).astype(BF16)

    u_next = _rmsnorm(xnext_ref[0], nmw_ref[...]).astype(BF16)
    umbuf[...] = u_next
    qkvbuf[...] = proj_of(u_next, OFF_Q, Q_WIDTH + 2 * KV_WIDTH)


def _whole(space=pltpu.VMEM):
    return pl.BlockSpec(memory_space=space)


def kernel(x, positions, norm_mix_w, w_in, b_in, conv_mix_w, w_conv_out, w_attn_out, b_attn_out,
           sinks, w_mix_out, norm_ffn_w, w_ffn_up, ffn_conv_w, ffn_conv_b, w_ffn_down, norm_final_w):
    b, s, d = x.shape
    depth = w_in.shape[0]
    tm = TOKEN_TILE
    assert d == D_MODEL and s % tm == 0 and tm % BLOCK == 0
    nj = s // tm
    n_tiles = b * nj

    def in_tile(t):
        return jnp.minimum(t, n_tiles - 1)

    x_spec = pl.BlockSpec((1, tm, D_MODEL), lambda t: (in_tile(t), 0, 0))
    xnext_spec = pl.BlockSpec((1, tm, D_MODEL), lambda t: (in_tile(t + 1), 0, 0))
    pos_spec = pl.BlockSpec((1, 1, tm), lambda t: (in_tile(t), 0, 0))
    out_spec = pl.BlockSpec((1, tm, D_MODEL), lambda t: (jnp.maximum(t - 1, 0), 0, 0))

    inv_freq = (ROPE_THETA ** (-jnp.arange(0, HEAD_DIM, 2, dtype=F32) / HEAD_DIM)).reshape(-1, 1)
    pos_tiles = positions.reshape(n_tiles, 1, tm)

    def rows(v):
        v = v.reshape(-1, v.shape[-1])
        return jnp.pad(v, ((0, 0), (0, IN_WIDTH - v.shape[-1])))

    def layer(final_norm):
        return pl.pallas_call(
            functools.partial(_block_kernel, nj, final_norm),
            grid=(n_tiles + 1,),
            in_specs=([_whole(pltpu.SMEM), pos_spec, x_spec, xnext_spec] + [_whole()] * 2
                      + [pl.BlockSpec(memory_space=pl.ANY)] * 6),
            out_specs=out_spec,
            out_shape=jax.ShapeDtypeStruct((n_tiles, tm, D_MODEL), F32),
            scratch_shapes=[
                pltpu.VMEM((HALO, D_MODEL), F32),
                pltpu.VMEM((4, BLOCK + tm, LANES), BF16),
                pltpu.VMEM((KV_WIDTH, BLOCK + tm), BF16),
                pltpu.VMEM((tm, Q_WIDTH), BF16),
                pltpu.VMEM((tm, Q_WIDTH), BF16),
                pltpu.VMEM((tm, D_MODEL), F32),
                pltpu.VMEM((tm, D_MODEL), BF16),
                pltpu.VMEM((tm, D_MODEL), BF16),
                pltpu.VMEM((tm, Q_WIDTH + 2 * KV_WIDTH), F32),
                pltpu.VMEM((HALO, D_FF), F32),
                pltpu.VMEM((tm, D_FF), BF16),
                pltpu.VMEM((D_MODEL, IN_WIDTH), BF16),
                pltpu.VMEM((D_MODEL, 2 * D_FF), BF16),
                pltpu.VMEM((D_MODEL, D_MODEL), BF16),
                pltpu.VMEM((D_MODEL, D_MODEL), BF16),
                pltpu.VMEM((Q_WIDTH, D_MODEL), BF16),
                pltpu.VMEM((D_FF, D_MODEL), BF16),
                pltpu.VMEM((STAGE_SLOTS, STAGE_ROWS_WIDE, IN_WIDTH), F32),
                pltpu.VMEM((STAGE_SLOTS, STAGE_ROWS_NARROW, D_MODEL), F32),
                pltpu.SemaphoreType.DMA((STAGE_SLOTS,)),
            ],
            compiler_params=pltpu.CompilerParams(dimension_semantics=("arbitrary",),
                                                 vmem_limit_bytes=VMEM_LIMIT_BYTES),
            name="decoder_block",
        )

    x = x.reshape(n_tiles, tm, D_MODEL)
    for l in range(depth):
        params = jnp.concatenate(
            [rows(b_in[l]), rows(norm_mix_w[l]), rows(b_attn_out[l]), rows(norm_ffn_w[l]),
             rows(norm_final_w), rows(conv_mix_w[l]), rows(ffn_conv_w[l]), rows(ffn_conv_b[l])],
            axis=0)
        x = layer(l == depth - 1)(
            sinks[l], pos_tiles, x, x, inv_freq, params, w_in[l], w_ffn_up[l], w_conv_out[l],
            w_mix_out[l], w_attn_out[l], w_ffn_down[l])
    return x.reshape(b, s, D_MODEL)
```

```python
import functools

import jax
import jax.numpy as jnp
from jax import lax
from jax.experimental import pallas as pl
from jax.experimental.pallas import tpu as pltpu

D_MODEL = 1024
HEAD_DIM = 64
N_Q_HEADS = 8
N_KV_HEADS = 2
BLOCK = 128
ROPE_THETA = 10000.0
Q_WIDTH = N_Q_HEADS * HEAD_DIM
KV_WIDTH = N_KV_HEADS * HEAD_DIM
D_FF = 2816
EPS = 1e-5

OFF_CB = 0
OFF_CC = OFF_CB + D_MODEL
OFF_CX = OFF_CC + D_MODEL
OFF_Q = OFF_CX + D_MODEL
OFF_K = OFF_Q + Q_WIDTH
OFF_V = OFF_K + KV_WIDTH
OFF_GC = OFF_V + KV_WIDTH
OFF_GA = OFF_GC + D_MODEL
IN_WIDTH = OFF_GA + D_MODEL

ROW_B_IN, ROW_NORM_MIX, ROW_B_ATTN_OUT, ROW_NORM_FFN, ROW_NORM_FINAL = 0, 1, 2, 3, 4
ROW_CONV_MIX, ROW_FFN_CONV, ROW_FFN_CONV_B = 5, 8, 11

LANES = 128
HALO = 8
TOKEN_TILE = 256
FF_CHUNKS = ((0, 1024), (1024, 2048), (2048, D_FF))
VMEM_LIMIT_BYTES = 60 * 1024 * 1024

STAGE_SLOTS = 4
STAGE_ROWS_WIDE = 32
STAGE_ROWS_NARROW = 128

F32 = jnp.float32
BF16 = jnp.bfloat16


def _rmsnorm(x, w):
    ms = jnp.mean(x * x, axis=-1, keepdims=True)
    return x * lax.rsqrt(ms + EPS) * w


def _dot(a, b):
    return jnp.dot(a, b, preferred_element_type=F32)


def _dot_nt(a, b):
    return lax.dot_general(a, b, (((1,), (1,)), ((), ())), preferred_element_type=F32)


def _causal_conv3(halo_ref, cur, taps_ref, c0, c1):
    tm = cur.shape[0]
    first_row = lax.broadcasted_iota(jnp.int32, (HALO, c1 - c0), 0) == 0

    def shift_down(v, halo_row):
        rolled = pltpu.roll(v, 1, 0)
        head = jnp.where(first_row, halo_row, rolled[0:HALO])
        return jnp.concatenate([head, rolled[HALO:]], axis=0)

    prev1 = shift_down(cur, halo_ref[HALO - 1:HALO, c0:c1])
    prev2 = shift_down(prev1, halo_ref[HALO - 2:HALO - 1, c0:c1])
    halo_ref[HALO - 2:HALO, c0:c1] = cur[tm - 2:tm]
    return (taps_ref[0:1, c0:c1] * prev2 + taps_ref[1:2, c0:c1] * prev1
            + taps_ref[2:3, c0:c1] * cur)


def _paired_columns(first, second, width, dst):
    moves = []
    for j in range(width // LANES):
        moves.append((first + j * LANES, dst + 2 * j * LANES, LANES))
        moves.append((second + j * LANES, dst + (2 * j + 1) * LANES, LANES))
    return moves


def _load_as_bf16(src_hbm, dst_ref, stage, sems, column_moves=None):
    rows, width = src_hbm.shape
    chunk = stage.shape[1]
    assert rows % chunk == 0 and width <= stage.shape[2]
    n = rows // chunk
    ahead = STAGE_SLOTS - 1
    moves = column_moves or [(0, 0, width)]
    assert sum(w for _, _, w in moves) == width

    def copy(i, slot):
        r0 = i * chunk if isinstance(i, int) else pl.multiple_of(i * chunk, chunk)
        return pltpu.make_async_copy(src_hbm.at[pl.ds(r0, chunk), :],
                                     stage.at[slot, :, 0:width], sems.at[slot])

    for i in range(min(ahead, n)):
        copy(i, i).start()

    def body(i, carry):
        slot = lax.rem(i, STAGE_SLOTS)
        copy(i, slot).wait()

        @pl.when(i + ahead < n)
        def _():
            copy(i + ahead, lax.rem(i + ahead, STAGE_SLOTS)).start()

        r0 = pl.multiple_of(i * chunk, chunk)
        for src_col, dst_col, w in moves:
            dst_ref[pl.ds(r0, chunk), dst_col:dst_col + w] = (
                stage[slot, :, src_col:src_col + w].astype(BF16))
        return carry

    lax.fori_loop(0, n, body, 0)


def _block_kernel(tiles_per_seq, final_norm,
                  sinks_ref, pos_ref, x_ref, xnext_ref, invf_ref, prm_ref, win_hbm, wup_hbm,
                  wco_hbm, wmo_hbm, wao_hbm, wdn_hbm, o_ref,
                  zbuf, kbuf, vbuf, qbuf, abuf, x1buf, u1buf, umbuf, qkvbuf, ubuf, hbuf,
                  win_ref, wup_ref, wco_ref, wmo_ref, wao_ref, wdn_ref,
                  stage_wide, stage_narrow, stage_sems):
    t = pl.program_id(0)
    tm = x_ref.shape[1]
    mixer_seq_start = lax.rem(t, tiles_per_seq) == 0
    ffn_seq_start = lax.rem(t + tiles_per_seq - 1, tiles_per_seq) == 0

    bin_ref = prm_ref.at[ROW_B_IN:ROW_B_IN + 1, :]
    nmw_ref = prm_ref.at[ROW_NORM_MIX:ROW_NORM_MIX + 1, 0:D_MODEL]
    bao_ref = prm_ref.at[ROW_B_ATTN_OUT:ROW_B_ATTN_OUT + 1, 0:D_MODEL]
    nfw_ref = prm_ref.at[ROW_NORM_FFN:ROW_NORM_FFN + 1, 0:D_MODEL]
    fin_ref = prm_ref.at[ROW_NORM_FINAL:ROW_NORM_FINAL + 1, 0:D_MODEL]
    cmw_ref = prm_ref.at[ROW_CONV_MIX:ROW_CONV_MIX + 3, 0:D_MODEL]
    fcw_ref = prm_ref.at[ROW_FFN_CONV:ROW_FFN_CONV + 3, 0:D_FF]
    fcb_ref = prm_ref.at[ROW_FFN_CONV_B:ROW_FFN_CONV_B + 1, 0:D_FF]

    @pl.when(t == 0)
    def _():
        x1buf[...] = jnp.zeros(x1buf.shape, F32)
        u1buf[...] = jnp.zeros(u1buf.shape, BF16)
        _load_as_bf16(win_hbm, win_ref, stage_wide, stage_sems,
                      [(0, 0, OFF_CC)] + _paired_columns(OFF_CC, OFF_CX, D_MODEL, OFF_CC)
                      + [(OFF_Q, OFF_Q, IN_WIDTH - OFF_Q)])
        _load_as_bf16(wup_hbm, wup_ref, stage_wide, stage_sems)
        _load_as_bf16(wco_hbm, wco_ref, stage_narrow, stage_sems)
        _load_as_bf16(wmo_hbm, wmo_ref, stage_narrow, stage_sems)
        _load_as_bf16(wao_hbm, wao_ref, stage_narrow, stage_sems)
        _load_as_bf16(wdn_hbm, wdn_ref, stage_narrow, stage_sems)
        u_first = _rmsnorm(x_ref[0], nmw_ref[...]).astype(BF16)
        umbuf[...] = u_first
        qkvbuf[...] = (_dot(u_first, win_ref[:, OFF_Q:OFF_GC]) + bin_ref[:, OFF_Q:OFF_GC])

    @pl.when(mixer_seq_start)
    def _():
        zbuf[0:HALO, :] = jnp.zeros((HALO, D_MODEL), F32)
        kbuf[:, 0:BLOCK, :] = jnp.zeros((4, BLOCK, LANES), BF16)
        vbuf[:, 0:BLOCK] = jnp.zeros((KV_WIDTH, BLOCK), BF16)

    @pl.when(jnp.logical_or(ffn_seq_start, t == 0))
    def _():
        ubuf[0:HALO, :] = jnp.zeros((HALO, D_FF), F32)

    x_prev = x1buf[...]
    u_prev = u1buf[...]
    x = x_ref[0]
    u = umbuf[...]

    def proj_of(lhs, off, width):
        return _dot(lhs, win_ref[:, off:off + width]) + bin_ref[:, off:off + width]

    def proj(off, width):
        return proj_of(u, off, width)

    def ffn_up(c0, c1):
        up = _dot(u_prev, wup_ref[:, c0:c1])
        a = _causal_conv3(ubuf, up, fcw_ref, c0, c1) + fcb_ref[:, c0:c1]
        gate = _dot(u_prev, wup_ref[:, D_FF + c0:D_FF + c1])
        hbuf[:, c0:c1] = (jax.nn.silu(a) * gate).astype(BF16)

    qkv = qkvbuf[...]
    ffn_up(*FF_CHUNKS[0])
    ffn_up(*FF_CHUNKS[1])

    ang = invf_ref[...] * pos_ref[0].astype(F32)
    cos_t, sin_t = jnp.cos(ang), jnp.sin(ang)
    cos = jnp.concatenate([cos_t] * 4, axis=0).T
    sin = jnp.concatenate([-sin_t, sin_t, -sin_t, sin_t], axis=0).T

    lane = lax.broadcasted_iota(jnp.int32, (tm, LANES), 1)
    first_half = (lane & (HEAD_DIM // 2)) == 0
    low_head = lane < HEAD_DIM

    def rope(v, c, s):
        partner = jnp.where(first_half, pltpu.roll(v, LANES - HEAD_DIM // 2, 1),
                            pltpu.roll(v, HEAD_DIM // 2, 1))
        return v * c + partner * s

    scale = HEAD_DIM ** -0.5
    cos_q, sin_q = cos * scale, sin * scale
    for c in range(Q_WIDTH // LANES):
        sl = slice(c * LANES, (c + 1) * LANES)
        qbuf[:, sl] = rope(qkv[:, sl], cos_q, sin_q).astype(BF16)

    k = rope(qkv[:, Q_WIDTH:Q_WIDTH + KV_WIDTH], cos, sin)
    krot = pltpu.roll(k, HEAD_DIM, 1)
    kzero = jnp.zeros_like(k)
    rows = slice(BLOCK, BLOCK + tm)
    kbuf[0, rows, :] = jnp.where(low_head, k, kzero).astype(BF16)
    kbuf[1, rows, :] = jnp.where(low_head, kzero, krot).astype(BF16)
    kbuf[2, rows, :] = jnp.where(low_head, krot, kzero).astype(BF16)
    kbuf[3, rows, :] = jnp.where(low_head, kzero, k).astype(BF16)
    vbuf[:, BLOCK:BLOCK + tm] = qkv[:, Q_WIDTH + KV_WIDTH:].T.astype(BF16)

    key2 = lax.broadcasted_iota(jnp.int32, (2 * BLOCK, 2 * BLOCK), 0)
    qry2 = lax.broadcasted_iota(jnp.int32, (2 * BLOCK, 2 * BLOCK), 1) & (BLOCK - 1)
    band = (key2 > qry2) & (key2 <= qry2 + BLOCK)
    neg_inf = jnp.full((2 * BLOCK, 2 * BLOCK), -jnp.inf, F32)
    zeros2 = jnp.zeros((2 * BLOCK, 2 * BLOCK), F32)
    bias = jnp.where(band, zeros2, neg_inf)
    bias_seq_start = jnp.where(band & (key2 >= BLOCK), zeros2, neg_inf)
    bias_first = jnp.where(mixer_seq_start, bias_seq_start, bias)

    ccx = _dot(u, win_ref[:, OFF_CC:OFF_Q])
    z = jnp.concatenate(
        [(ccx[:, 2 * j * LANES:(2 * j + 1) * LANES]
          + bin_ref[:, OFF_CC + j * LANES:OFF_CC + (j + 1) * LANES])
         * (ccx[:, (2 * j + 1) * LANES:(2 * j + 2) * LANES]
            + bin_ref[:, OFF_CX + j * LANES:OFF_CX + (j + 1) * LANES])
         for j in range(D_MODEL // LANES)], axis=1)

    blocks = [(i, g) for i in range(tm // BLOCK) for g in range(N_KV_HEADS)]
    scores = {}
    for i, g in blocks:
        qrows = slice(i * BLOCK, (i + 1) * BLOCK)
        krows = slice(i * BLOCK, (i + 2) * BLOCK)
        blk_bias = bias_first if i == 0 else bias
        c0 = g * 2 * LANES
        qpair = jnp.concatenate([qbuf[qrows, c0:c0 + LANES],
                                 qbuf[qrows, c0 + LANES:c0 + 2 * LANES]], axis=0)
        scores[i, g] = (_dot_nt(kbuf[2 * g, krows, :], qpair) + blk_bias,
                        _dot_nt(kbuf[2 * g + 1, krows, :], qpair) + blk_bias)

    ffn_up(*FF_CHUNKS[2])

    top_pair = lax.broadcasted_iota(jnp.int32, (1, 2 * BLOCK), 1) < BLOCK
    sum_row = lax.broadcasted_iota(jnp.int32, (2 * HALO, 4 * BLOCK), 0)
    sum_col = lax.broadcasted_iota(jnp.int32, (2 * HALO, 4 * BLOCK), 1)
    sum_rows = jnp.where(sum_row == (sum_col >= 2 * BLOCK).astype(jnp.int32), 1.0, 0.0)
    sum_rows = sum_rows.astype(BF16)
    vzero = jnp.zeros((HEAD_DIM, 2 * BLOCK), BF16)

    for i, g in blocks:
        qrows = slice(i * BLOCK, (i + 1) * BLOCK)
        kcols = slice(i * BLOCK, (i + 2) * BLOCK)
        c0 = g * 2 * LANES
        s_even, s_odd = scores[i, g]
        sink_even = jnp.where(top_pair, sinks_ref[4 * g], sinks_ref[4 * g + 2])
        sink_odd = jnp.where(top_pair, sinks_ref[4 * g + 1], sinks_ref[4 * g + 3])
        m_even = jnp.maximum(jnp.max(s_even, axis=0, keepdims=True), sink_even)
        m_odd = jnp.maximum(jnp.max(s_odd, axis=0, keepdims=True), sink_odd)
        p_t = jnp.concatenate([jnp.exp(s_even - m_even).astype(BF16),
                               jnp.exp(s_odd - m_odd).astype(BF16)], axis=0)
        v_t = vbuf[g * HEAD_DIM:(g + 1) * HEAD_DIM, kcols]
        lhs = jnp.concatenate([jnp.concatenate([v_t, vzero], axis=1),
                               jnp.concatenate([vzero, v_t], axis=1), sum_rows], axis=0)
        r = _dot(lhs, p_t)
        inv_even = 1.0 / (r[2 * HEAD_DIM:2 * HEAD_DIM + 1] + jnp.exp(sink_even - m_even))
        inv_odd = 1.0 / (r[2 * HEAD_DIM + 1:2 * HEAD_DIM + 2] + jnp.exp(sink_odd - m_odd))
        out = jnp.concatenate([r[0:HEAD_DIM] * inv_even,
                               r[HEAD_DIM:2 * HEAD_DIM] * inv_odd], axis=0).T.astype(BF16)
        abuf[qrows, c0:c0 + LANES] = out[:BLOCK]
        abuf[qrows, c0 + LANES:c0 + 2 * LANES] = out[BLOCK:]

    kbuf[:, 0:BLOCK, :] = kbuf[:, tm:tm + BLOCK, :]
    vbuf[:, 0:BLOCK] = vbuf[:, tm:tm + BLOCK]

    conv = _causal_conv3(zbuf, z, cmw_ref, 0, D_MODEL)
    gated = (proj(OFF_CB, D_MODEL) * conv).astype(BF16)
    gate_conv = proj(OFF_GC, D_MODEL)
    gate_attn = proj(OFF_GA, D_MODEL)
    y_conv = _dot(gated, wco_ref[...])

    y_attn = _dot(abuf[...], wao_ref[...]) + bao_ref[...]

    y = x_prev + _dot(hbuf[...], wdn_ref[...])
    o_ref[0] = _rmsnorm(y, fin_ref[...]) if final_norm else y

    merged = jax.nn.sigmoid(gate_conv) * y_conv + jax.nn.sigmoid(gate_attn) * y_attn
    x1 = x + _dot(merged.astype(BF16), wmo_ref[...])
    x1buf[...] = x1
    u1buf[...] = _rmsnorm(x1, nfw_ref[...]).astype(BF16)

    u_next = _rmsnorm(xnext_ref[0], nmw_ref[...]).astype(BF16)
    umbuf[...] = u_next
    qkvbuf[...] = proj_of(u_next, OFF_Q, Q_WIDTH + 2 * KV_WIDTH)


def _whole(space=pltpu.VMEM):
    return pl.BlockSpec(memory_space=space)


def kernel(x, positions, norm_mix_w, w_in, b_in, conv_mix_w, w_conv_out, w_attn_out, b_attn_out,
           sinks, w_mix_out, norm_ffn_w, w_ffn_up, ffn_conv_w, ffn_conv_b, w_ffn_down, norm_final_w):
    b, s, d = x.shape
    depth = w_in.shape[0]
    tm = TOKEN_TILE
    assert d == D_MODEL and s % tm == 0 and tm % BLOCK == 0
    nj = s // tm
    n_tiles = b * nj

    def in_tile(t):
        return jnp.minimum(t, n_tiles - 1)

    x_spec = pl.BlockSpec((1, tm, D_MODEL), lambda t: (in_tile(t), 0, 0))
    xnext_spec = pl.BlockSpec((1, tm, D_MODEL), lambda t: (in_tile(t + 1), 0, 0))
    pos_spec = pl.BlockSpec((1, 1, tm), lambda t: (in_tile(t), 0, 0))
    out_spec = pl.BlockSpec((1, tm, D_MODEL), lambda t: (jnp.maximum(t - 1, 0), 0, 0))

    inv_freq = (ROPE_THETA ** (-jnp.arange(0, HEAD_DIM, 2, dtype=F32) / HEAD_DIM)).reshape(-1, 1)
    pos_tiles = positions.reshape(n_tiles, 1, tm)

    def rows(v):
        v = v.reshape(-1, v.shape[-1])
        return jnp.pad(v, ((0, 0), (0, IN_WIDTH - v.shape[-1])))

    def layer(final_norm):
        return pl.pallas_call(
            functools.partial(_block_kernel, nj, final_norm),
            grid=(n_tiles + 1,),
            in_specs=([_whole(pltpu.SMEM), pos_spec, x_spec, xnext_spec] + [_whole()] * 2
                      + [pl.BlockSpec(memory_space=pl.ANY)] * 6),
            out_specs=out_spec,
            out_shape=jax.ShapeDtypeStruct((n_tiles, tm, D_MODEL), F32),
            scratch_shapes=[
                pltpu.VMEM((HALO, D_MODEL), F32),
                pltpu.VMEM((4, BLOCK + tm, LANES), BF16),
                pltpu.VMEM((KV_WIDTH, BLOCK + tm), BF16),
                pltpu.VMEM((tm, Q_WIDTH), BF16),
                pltpu.VMEM((tm, Q_WIDTH), BF16),
                pltpu.VMEM((tm, D_MODEL), F32),
                pltpu.VMEM((tm, D_MODEL), BF16),
                pltpu.VMEM((tm, D_MODEL), BF16),
                pltpu.VMEM((tm, Q_WIDTH + 2 * KV_WIDTH), F32),
                pltpu.VMEM((HALO, D_FF), F32),
                pltpu.VMEM((tm, D_FF), BF16),
                pltpu.VMEM((D_MODEL, IN_WIDTH), BF16),
                pltpu.VMEM((D_MODEL, 2 * D_FF), BF16),
                pltpu.VMEM((D_MODEL, D_MODEL), BF16),
                pltpu.VMEM((D_MODEL, D_MODEL), BF16),
                pltpu.VMEM((Q_WIDTH, D_MODEL), BF16),
                pltpu.VMEM((D_FF, D_MODEL), BF16),
                pltpu.VMEM((STAGE_SLOTS, STAGE_ROWS_WIDE, IN_WIDTH), F32),
                pltpu.VMEM((STAGE_SLOTS, STAGE_ROWS_NARROW, D_MODEL), F32),
                pltpu.SemaphoreType.DMA((STAGE_SLOTS,)),
            ],
            compiler_params=pltpu.CompilerParams(dimension_semantics=("arbitrary",),
                                                 vmem_limit_bytes=VMEM_LIMIT_BYTES),
            name="decoder_block",
        )

    x = x.reshape(n_tiles, tm, D_MODEL)
    for l in range(depth):
        params = jnp.concatenate(
            [rows(b_in[l]), rows(norm_mix_w[l]), rows(b_attn_out[l]), rows(norm_ffn_w[l]),
             rows(norm_final_w), rows(conv_mix_w[l]), rows(ffn_conv_w[l]), rows(ffn_conv_b[l])],
            axis=0)
        x = layer(l == depth - 1)(
            sinks[l], pos_tiles, x, x, inv_freq, params, w_in[l], w_ffn_up[l], w_conv_out[l],
            w_mix_out[l], w_attn_out[l], w_ffn_down[l])
    return x.reshape(b, s, D_MODEL)
```

```python
import functools

import jax
import jax.numpy as jnp
from jax import lax
from jax.experimental import pallas as pl
from jax.experimental.pallas import tpu as pltpu

D_MODEL = 1024
HEAD_DIM = 64
N_Q_HEADS = 8
N_KV_HEADS = 2
BLOCK = 128
ROPE_THETA = 10000.0
Q_WIDTH = N_Q_HEADS * HEAD_DIM
KV_WIDTH = N_KV_HEADS * HEAD_DIM
D_FF = 2816
EPS = 1e-5

OFF_CB = 0
OFF_CC = OFF_CB + D_MODEL
OFF_CX = OFF_CC + D_MODEL
OFF_Q = OFF_CX + D_MODEL
OFF_K = OFF_Q + Q_WIDTH
OFF_V = OFF_K + KV_WIDTH
OFF_GC = OFF_V + KV_WIDTH
OFF_GA = OFF_GC + D_MODEL
IN_WIDTH = OFF_GA + D_MODEL

ROW_B_IN, ROW_NORM_MIX, ROW_B_ATTN_OUT, ROW_NORM_FFN, ROW_NORM_FINAL = 0, 1, 2, 3, 4
ROW_CONV_MIX, ROW_FFN_CONV, ROW_FFN_CONV_B = 5, 8, 11

LANES = 128
HALO = 8
TOKEN_TILE = 256
FF_CHUNKS = ((0, 1536), (1536, D_FF))
VMEM_LIMIT_BYTES = 60 * 1024 * 1024

STAGE_SLOTS = 4
STAGE_ROWS_WIDE = 32
STAGE_ROWS_NARROW = 128

F32 = jnp.float32
BF16 = jnp.bfloat16


def _rmsnorm(x, w):
    ms = jnp.mean(x * x, axis=-1, keepdims=True)
    return x * lax.rsqrt(ms + EPS) * w


def _dot(a, b):
    return jnp.dot(a, b, preferred_element_type=F32)


def _dot_nt(a, b):
    return lax.dot_general(a, b, (((1,), (1,)), ((), ())), preferred_element_type=F32)


def _causal_conv3(halo_ref, cur, taps_ref, c0, c1):
    tm = cur.shape[0]
    first_row = lax.broadcasted_iota(jnp.int32, (HALO, c1 - c0), 0) == 0

    def shift_down(v, halo_row):
        rolled = pltpu.roll(v, 1, 0)
        head = jnp.where(first_row, halo_row, rolled[0:HALO])
        return jnp.concatenate([head, rolled[HALO:]], axis=0)

    prev1 = shift_down(cur, halo_ref[HALO - 1:HALO, c0:c1])
    prev2 = shift_down(prev1, halo_ref[HALO - 2:HALO - 1, c0:c1])
    halo_ref[HALO - 2:HALO, c0:c1] = cur[tm - 2:tm]
    return (taps_ref[0:1, c0:c1] * prev2 + taps_ref[1:2, c0:c1] * prev1
            + taps_ref[2:3, c0:c1] * cur)


def _load_as_bf16(src_hbm, dst_ref, stage, sems):
    rows, width = src_hbm.shape
    chunk = stage.shape[1]
    assert rows % chunk == 0 and width <= stage.shape[2]
    n = rows // chunk
    ahead = STAGE_SLOTS - 1

    def copy(i, slot):
        r0 = i * chunk if isinstance(i, int) else pl.multiple_of(i * chunk, chunk)
        return pltpu.make_async_copy(src_hbm.at[pl.ds(r0, chunk), :],
                                     stage.at[slot, :, 0:width], sems.at[slot])

    for i in range(min(ahead, n)):
        copy(i, i).start()

    def body(i, carry):
        slot = lax.rem(i, STAGE_SLOTS)
        copy(i, slot).wait()

        @pl.when(i + ahead < n)
        def _():
            copy(i + ahead, lax.rem(i + ahead, STAGE_SLOTS)).start()

        r0 = pl.multiple_of(i * chunk, chunk)
        dst_ref[pl.ds(r0, chunk), :] = stage[slot, :, 0:width].astype(BF16)
        return carry

    lax.fori_loop(0, n, body, 0)


def _block_kernel(tiles_per_seq, final_norm,
                  sinks_ref, pos_ref, x_ref, xnext_ref, invf_ref, prm_ref, win_hbm, wup_hbm,
                  wco_hbm, wmo_hbm, wao_hbm, wdn_hbm, o_ref,
                  zbuf, kbuf, vbuf, qbuf, abuf, x1buf, u1buf, umbuf, qkvbuf, ubuf, hbuf,
                  win_ref, wup_ref, wco_ref, wmo_ref, wao_ref, wdn_ref,
                  stage_wide, stage_narrow, stage_sems):
    t = pl.program_id(0)
    tm = x_ref.shape[1]
    mixer_seq_start = lax.rem(t, tiles_per_seq) == 0
    ffn_seq_start = lax.rem(t + tiles_per_seq - 1, tiles_per_seq) == 0

    bin_ref = prm_ref.at[ROW_B_IN:ROW_B_IN + 1, :]
    nmw_ref = prm_ref.at[ROW_NORM_MIX:ROW_NORM_MIX + 1, 0:D_MODEL]
    bao_ref = prm_ref.at[ROW_B_ATTN_OUT:ROW_B_ATTN_OUT + 1, 0:D_MODEL]
    nfw_ref = prm_ref.at[ROW_NORM_FFN:ROW_NORM_FFN + 1, 0:D_MODEL]
    fin_ref = prm_ref.at[ROW_NORM_FINAL:ROW_NORM_FINAL + 1, 0:D_MODEL]
    cmw_ref = prm_ref.at[ROW_CONV_MIX:ROW_CONV_MIX + 3, 0:D_MODEL]
    fcw_ref = prm_ref.at[ROW_FFN_CONV:ROW_FFN_CONV + 3, 0:D_FF]
    fcb_ref = prm_ref.at[ROW_FFN_CONV_B:ROW_FFN_CONV_B + 1, 0:D_FF]

    @pl.when(t == 0)
    def _():
        x1buf[...] = jnp.zeros(x1buf.shape, F32)
        u1buf[...] = jnp.zeros(u1buf.shape, BF16)
        _load_as_bf16(win_hbm, win_ref, stage_wide, stage_sems)
        _load_as_bf16(wup_hbm, wup_ref, stage_wide, stage_sems)
        _load_as_bf16(wco_hbm, wco_ref, stage_narrow, stage_sems)
        _load_as_bf16(wmo_hbm, wmo_ref, stage_narrow, stage_sems)
        _load_as_bf16(wao_hbm, wao_ref, stage_narrow, stage_sems)
        _load_as_bf16(wdn_hbm, wdn_ref, stage_narrow, stage_sems)
        u_first = _rmsnorm(x_ref[0], nmw_ref[...]).astype(BF16)
        umbuf[...] = u_first
        qkvbuf[...] = (_dot(u_first, win_ref[:, OFF_Q:OFF_GC]) + bin_ref[:, OFF_Q:OFF_GC])

    @pl.when(mixer_seq_start)
    def _():
        zbuf[0:HALO, :] = jnp.zeros((HALO, D_MODEL), F32)
        kbuf[:, 0:BLOCK, :] = jnp.zeros((4, BLOCK, LANES), BF16)
        vbuf[:, 0:BLOCK] = jnp.zeros((KV_WIDTH, BLOCK), BF16)

    @pl.when(jnp.logical_or(ffn_seq_start, t == 0))
    def _():
        ubuf[0:HALO, :] = jnp.zeros((HALO, D_FF), F32)

    x_prev = x1buf[...]
    u_prev = u1buf[...]
    x = x_ref[0]
    u = umbuf[...]

    def proj_of(lhs, off, width):
        return _dot(lhs, win_ref[:, off:off + width]) + bin_ref[:, off:off + width]

    def proj(off, width):
        return proj_of(u, off, width)

    def ffn_up(c0, c1):
        up = _dot(u_prev, wup_ref[:, c0:c1])
        a = _causal_conv3(ubuf, up, fcw_ref, c0, c1) + fcb_ref[:, c0:c1]
        gate = _dot(u_prev, wup_ref[:, D_FF + c0:D_FF + c1])
        hbuf[:, c0:c1] = (jax.nn.silu(a) * gate).astype(BF16)

    qkv = qkvbuf[...]
    ffn_up(*FF_CHUNKS[0])

    ang = invf_ref[...] * pos_ref[0].astype(F32)
    cos_t, sin_t = jnp.cos(ang), jnp.sin(ang)
    cos = jnp.concatenate([cos_t] * 4, axis=0).T
    sin = jnp.concatenate([-sin_t, sin_t, -sin_t, sin_t], axis=0).T

    lane = lax.broadcasted_iota(jnp.int32, (tm, LANES), 1)
    first_half = (lane & (HEAD_DIM // 2)) == 0
    low_head = lane < HEAD_DIM

    def rope(v, c, s):
        partner = jnp.where(first_half, pltpu.roll(v, LANES - HEAD_DIM // 2, 1),
                            pltpu.roll(v, HEAD_DIM // 2, 1))
        return v * c + partner * s

    scale = HEAD_DIM ** -0.5
    cos_q, sin_q = cos * scale, sin * scale
    for c in range(Q_WIDTH // LANES):
        sl = slice(c * LANES, (c + 1) * LANES)
        qbuf[:, sl] = rope(qkv[:, sl], cos_q, sin_q).astype(BF16)

    k = rope(qkv[:, Q_WIDTH:Q_WIDTH + KV_WIDTH], cos, sin)
    krot = pltpu.roll(k, HEAD_DIM, 1)
    kzero = jnp.zeros_like(k)
    rows = slice(BLOCK, BLOCK + tm)
    kbuf[0, rows, :] = jnp.where(low_head, k, kzero).astype(BF16)
    kbuf[1, rows, :] = jnp.where(low_head, kzero, krot).astype(BF16)
    kbuf[2, rows, :] = jnp.where(low_head, krot, kzero).astype(BF16)
    kbuf[3, rows, :] = jnp.where(low_head, kzero, k).astype(BF16)
    vbuf[:, BLOCK:BLOCK + tm] = qkv[:, Q_WIDTH + KV_WIDTH:].T.astype(BF16)

    key2 = lax.broadcasted_iota(jnp.int32, (2 * BLOCK, 2 * BLOCK), 0)
    qry2 = lax.broadcasted_iota(jnp.int32, (2 * BLOCK, 2 * BLOCK), 1) & (BLOCK - 1)
    band = (key2 > qry2) & (key2 <= qry2 + BLOCK)
    neg_inf = jnp.full((2 * BLOCK, 2 * BLOCK), -jnp.inf, F32)
    zeros2 = jnp.zeros((2 * BLOCK, 2 * BLOCK), F32)
    bias = jnp.where(band, zeros2, neg_inf)
    bias_seq_start = jnp.where(band & (key2 >= BLOCK), zeros2, neg_inf)
    bias_first = jnp.where(mixer_seq_start, bias_seq_start, bias)

    z = proj(OFF_CC, D_MODEL) * proj(OFF_CX, D_MODEL)

    blocks = [(i, g) for i in range(tm // BLOCK) for g in range(N_KV_HEADS)]
    scores = {}
    for i, g in blocks:
        qrows = slice(i * BLOCK, (i + 1) * BLOCK)
        krows = slice(i * BLOCK, (i + 2) * BLOCK)
        blk_bias = bias_first if i == 0 else bias
        c0 = g * 2 * LANES
        qpair = jnp.concatenate([qbuf[qrows, c0:c0 + LANES],
                                 qbuf[qrows, c0 + LANES:c0 + 2 * LANES]], axis=0)
        scores[i, g] = (_dot_nt(kbuf[2 * g, krows, :], qpair) + blk_bias,
                        _dot_nt(kbuf[2 * g + 1, krows, :], qpair) + blk_bias)

    ffn_up(*FF_CHUNKS[1])

    top_pair = lax.broadcasted_iota(jnp.int32, (1, 2 * BLOCK), 1) < BLOCK
    sum_row = lax.broadcasted_iota(jnp.int32, (2 * HALO, 4 * BLOCK), 0)
    sum_col = lax.broadcasted_iota(jnp.int32, (2 * HALO, 4 * BLOCK), 1)
    sum_rows = jnp.where(sum_row == (sum_col >= 2 * BLOCK).astype(jnp.int32), 1.0, 0.0)
    sum_rows = sum_rows.astype(BF16)
    vzero = jnp.zeros((HEAD_DIM, 2 * BLOCK), BF16)

    for i, g in blocks:
        qrows = slice(i * BLOCK, (i + 1) * BLOCK)
        kcols = slice(i * BLOCK, (i + 2) * BLOCK)
        c0 = g * 2 * LANES
        s_even, s_odd = scores[i, g]
        sink_even = jnp.where(top_pair, sinks_ref[4 * g], sinks_ref[4 * g + 2])
        sink_odd = jnp.where(top_pair, sinks_ref[4 * g + 1], sinks_ref[4 * g + 3])
        m_even = jnp.maximum(jnp.max(s_even, axis=0, keepdims=True), sink_even)
        m_odd = jnp.maximum(jnp.max(s_odd, axis=0, keepdims=True), sink_odd)
        p_t = jnp.concatenate([jnp.exp(s_even - m_even).astype(BF16),
                               jnp.exp(s_odd - m_odd).astype(BF16)], axis=0)
        v_t = vbuf[g * HEAD_DIM:(g + 1) * HEAD_DIM, kcols]
        lhs = jnp.concatenate([jnp.concatenate([v_t, vzero], axis=1),
                               jnp.concatenate([vzero, v_t], axis=1), sum_rows], axis=0)
        r = _dot(lhs, p_t)
        inv_even = 1.0 / (r[2 * HEAD_DIM:2 * HEAD_DIM + 1] + jnp.exp(sink_even - m_even))
        inv_odd = 1.0 / (r[2 * HEAD_DIM + 1:2 * HEAD_DIM + 2] + jnp.exp(sink_odd - m_odd))
        out = jnp.concatenate([r[0:HEAD_DIM] * inv_even,
                               r[HEAD_DIM:2 * HEAD_DIM] * inv_odd], axis=0).T.astype(BF16)
        abuf[qrows, c0:c0 + LANES] = out[:BLOCK]
        abuf[qrows, c0 + LANES:c0 + 2 * LANES] = out[BLOCK:]

    kbuf[:, 0:BLOCK, :] = kbuf[:, tm:tm + BLOCK, :]
    vbuf[:, 0:BLOCK] = vbuf[:, tm:tm + BLOCK]

    conv = _causal_conv3(zbuf, z, cmw_ref, 0, D_MODEL)
    gated = (proj(OFF_CB, D_MODEL) * conv).astype(BF16)
    gate_conv = proj(OFF_GC, D_MODEL)
    gate_attn = proj(OFF_GA, D_MODEL)
    y_conv = _dot(gated, wco_ref[...])

    y_attn = _dot(abuf[...], wao_ref[...]) + bao_ref[...]

    y = x_prev + _dot(hbuf[...], wdn_ref[...])
    o_ref[0] = _rmsnorm(y, fin_ref[...]) if final_norm else y

    merged = jax.nn.sigmoid(gate_conv) * y_conv + jax.nn.sigmoid(gate_attn) * y_attn
    x1 = x + _dot(merged.astype(BF16), wmo_ref[...])
    x1buf[...] = x1
    u1buf[...] = _rmsnorm(x1, nfw_ref[...]).astype(BF16)

    u_next = _rmsnorm(xnext_ref[0], nmw_ref[...]).astype(BF16)
    umbuf[...] = u_next
    qkvbuf[...] = proj_of(u_next, OFF_Q, Q_WIDTH + 2 * KV_WIDTH)


def _whole(space=pltpu.VMEM):
    return pl.BlockSpec(memory_space=space)


def kernel(x, positions, norm_mix_w, w_in, b_in, conv_mix_w, w_conv_out, w_attn_out, b_attn_out,
           sinks, w_mix_out, norm_ffn_w, w_ffn_up, ffn_conv_w, ffn_conv_b, w_ffn_down, norm_final_w):
    b, s, d = x.shape
    depth = w_in.shape[0]
    tm = TOKEN_TILE
    assert d == D_MODEL and s % tm == 0 and tm % BLOCK == 0
    nj = s // tm
    n_tiles = b * nj

    def in_tile(t):
        return jnp.minimum(t, n_tiles - 1)

    x_spec = pl.BlockSpec((1, tm, D_MODEL), lambda t: (in_tile(t), 0, 0))
    xnext_spec = pl.BlockSpec((1, tm, D_MODEL), lambda t: (in_tile(t + 1), 0, 0))
    pos_spec = pl.BlockSpec((1, 1, tm), lambda t: (in_tile(t), 0, 0))
    out_spec = pl.BlockSpec((1, tm, D_MODEL), lambda t: (jnp.maximum(t - 1, 0), 0, 0))

    inv_freq = (ROPE_THETA ** (-jnp.arange(0, HEAD_DIM, 2, dtype=F32) / HEAD_DIM)).reshape(-1, 1)
    pos_tiles = positions.reshape(n_tiles, 1, tm)

    def rows(v):
        v = v.reshape(-1, v.shape[-1])
        return jnp.pad(v, ((0, 0), (0, IN_WIDTH - v.shape[-1])))

    def layer(final_norm):
        return pl.pallas_call(
            functools.partial(_block_kernel, nj, final_norm),
            grid=(n_tiles + 1,),
            in_specs=([_whole(pltpu.SMEM), pos_spec, x_spec, xnext_spec] + [_whole()] * 2
                      + [pl.BlockSpec(memory_space=pl.ANY)] * 6),
            out_specs=out_spec,
            out_shape=jax.ShapeDtypeStruct((n_tiles, tm, D_MODEL), F32),
            scratch_shapes=[
                pltpu.VMEM((HALO, D_MODEL), F32),
                pltpu.VMEM((4, BLOCK + tm, LANES), BF16),
                pltpu.VMEM((KV_WIDTH, BLOCK + tm), BF16),
                pltpu.VMEM((tm, Q_WIDTH), BF16),
                pltpu.VMEM((tm, Q_WIDTH), BF16),
                pltpu.VMEM((tm, D_MODEL), F32),
                pltpu.VMEM((tm, D_MODEL), BF16),
                pltpu.VMEM((tm, D_MODEL), BF16),
                pltpu.VMEM((tm, Q_WIDTH + 2 * KV_WIDTH), F32),
                pltpu.VMEM((HALO, D_FF), F32),
                pltpu.VMEM((tm, D_FF), BF16),
                pltpu.VMEM((D_MODEL, IN_WIDTH), BF16),
                pltpu.VMEM((D_MODEL, 2 * D_FF), BF16),
                pltpu.VMEM((D_MODEL, D_MODEL), BF16),
                pltpu.VMEM((D_MODEL, D_MODEL), BF16),
                pltpu.VMEM((Q_WIDTH, D_MODEL), BF16),
                pltpu.VMEM((D_FF, D_MODEL), BF16),
                pltpu.VMEM((STAGE_SLOTS, STAGE_ROWS_WIDE, IN_WIDTH), F32),
                pltpu.VMEM((STAGE_SLOTS, STAGE_ROWS_NARROW, D_MODEL), F32),
                pltpu.SemaphoreType.DMA((STAGE_SLOTS,)),
            ],
            compiler_params=pltpu.CompilerParams(dimension_semantics=("arbitrary",),
                                                 vmem_limit_bytes=VMEM_LIMIT_BYTES),
            name="decoder_block",
        )

    x = x.reshape(n_tiles, tm, D_MODEL)
    for l in range(depth):
        params = jnp.concatenate(
            [rows(b_in[l]), rows(norm_mix_w[l]), rows(b_attn_out[l]), rows(norm_ffn_w[l]),
             rows(norm_final_w), rows(conv_mix_w[l]), rows(ffn_conv_w[l]), rows(ffn_conv_b[l])],
            axis=0)
        x = layer(l == depth - 1)(
            sinks[l], pos_tiles, x, x, inv_freq, params, w_in[l], w_ffn_up[l], w_conv_out[l],
            w_mix_out[l], w_attn_out[l], w_ffn_down[l])
    return x.reshape(b, s, D_MODEL)
```

```python
import functools

import jax
import jax.numpy as jnp
from jax import lax
from jax.experimental import pallas as pl
from jax.experimental.pallas import tpu as pltpu

D_MODEL = 1024
HEAD_DIM = 64
N_Q_HEADS = 8
N_KV_HEADS = 2
BLOCK = 128
ROPE_THETA = 10000.0
Q_WIDTH = N_Q_HEADS * HEAD_DIM
KV_WIDTH = N_KV_HEADS * HEAD_DIM
D_FF = 2816
EPS = 1e-5

OFF_CB = 0
OFF_CC = OFF_CB + D_MODEL
OFF_CX = OFF_CC + D_MODEL
OFF_Q = OFF_CX + D_MODEL
OFF_K = OFF_Q + Q_WIDTH
OFF_V = OFF_K + KV_WIDTH
OFF_GC = OFF_V + KV_WIDTH
OFF_GA = OFF_GC + D_MODEL
IN_WIDTH = OFF_GA + D_MODEL

ROW_B_IN, ROW_NORM_MIX, ROW_B_ATTN_OUT, ROW_NORM_FFN, ROW_NORM_FINAL = 0, 1, 2, 3, 4
ROW_CONV_MIX, ROW_FFN_CONV, ROW_FFN_CONV_B = 5, 8, 11

LANES = 128
HALO = 8
TOKEN_TILE = 256
FF_CHUNKS = ((0, 1024), (1024, 2048), (2048, D_FF))
VMEM_LIMIT_BYTES = 60 * 1024 * 1024

STAGE_SLOTS = 4
STAGE_ROWS_WIDE = 32
STAGE_ROWS_NARROW = 128

F32 = jnp.float32
BF16 = jnp.bfloat16


def _rmsnorm(x, w):
    ms = jnp.mean(x * x, axis=-1, keepdims=True)
    return x * lax.rsqrt(ms + EPS) * w


def _dot(a, b):
    return jnp.dot(a, b, preferred_element_type=F32)


def _dot_nt(a, b):
    return lax.dot_general(a, b, (((1,), (1,)), ((), ())), preferred_element_type=F32)


def _causal_conv3(halo_ref, cur, taps_ref, c0, c1):
    tm = cur.shape[0]
    first_row = lax.broadcasted_iota(jnp.int32, (HALO, c1 - c0), 0) == 0

    def shift_down(v, halo_row):
        rolled = pltpu.roll(v, 1, 0)
        head = jnp.where(first_row, halo_row, rolled[0:HALO])
        return jnp.concatenate([head, rolled[HALO:]], axis=0)

    prev1 = shift_down(cur, halo_ref[HALO - 1:HALO, c0:c1])
    prev2 = shift_down(prev1, halo_ref[HALO - 2:HALO - 1, c0:c1])
    halo_ref[HALO - 2:HALO, c0:c1] = cur[tm - 2:tm]
    return (taps_ref[0:1, c0:c1] * prev2 + taps_ref[1:2, c0:c1] * prev1
            + taps_ref[2:3, c0:c1] * cur)


def _load_as_bf16(src_hbm, dst_ref, stage, sems):
    rows, width = src_hbm.shape
    chunk = stage.shape[1]
    assert rows % chunk == 0 and width <= stage.shape[2]
    n = rows // chunk
    ahead = STAGE_SLOTS - 1

    def copy(i, slot):
        r0 = i * chunk if isinstance(i, int) else pl.multiple_of(i * chunk, chunk)
        return pltpu.make_async_copy(src_hbm.at[pl.ds(r0, chunk), :],
                                     stage.at[slot, :, 0:width], sems.at[slot])

    for i in range(min(ahead, n)):
        copy(i, i).start()

    def body(i, carry):
        slot = lax.rem(i, STAGE_SLOTS)
        copy(i, slot).wait()

        @pl.when(i + ahead < n)
        def _():
            copy(i + ahead, lax.rem(i + ahead, STAGE_SLOTS)).start()

        r0 = pl.multiple_of(i * chunk, chunk)
        dst_ref[pl.ds(r0, chunk), :] = stage[slot, :, 0:width].astype(BF16)
        return carry

    lax.fori_loop(0, n, body, 0)


def _block_kernel(tiles_per_seq, final_norm,
                  sinks_ref, pos_ref, x_ref, invf_ref, prm_ref, win_hbm, wup_hbm, wco_hbm,
                  wmo_hbm, wao_hbm, wdn_hbm, o_ref,
                  zbuf, kbuf, vbuf, qbuf, abuf, x1buf, u1buf, ubuf, hbuf,
                  win_ref, wup_ref, wco_ref, wmo_ref, wao_ref, wdn_ref,
                  stage_wide, stage_narrow, stage_sems):
    t = pl.program_id(0)
    tm = x_ref.shape[1]
    mixer_seq_start = lax.rem(t, tiles_per_seq) == 0
    ffn_seq_start = lax.rem(t + tiles_per_seq - 1, tiles_per_seq) == 0

    bin_ref = prm_ref.at[ROW_B_IN:ROW_B_IN + 1, :]
    nmw_ref = prm_ref.at[ROW_NORM_MIX:ROW_NORM_MIX + 1, 0:D_MODEL]
    bao_ref = prm_ref.at[ROW_B_ATTN_OUT:ROW_B_ATTN_OUT + 1, 0:D_MODEL]
    nfw_ref = prm_ref.at[ROW_NORM_FFN:ROW_NORM_FFN + 1, 0:D_MODEL]
    fin_ref = prm_ref.at[ROW_NORM_FINAL:ROW_NORM_FINAL + 1, 0:D_MODEL]
    cmw_ref = prm_ref.at[ROW_CONV_MIX:ROW_CONV_MIX + 3, 0:D_MODEL]
    fcw_ref = prm_ref.at[ROW_FFN_CONV:ROW_FFN_CONV + 3, 0:D_FF]
    fcb_ref = prm_ref.at[ROW_FFN_CONV_B:ROW_FFN_CONV_B + 1, 0:D_FF]

    @pl.when(t == 0)
    def _():
        x1buf[...] = jnp.zeros(x1buf.shape, F32)
        u1buf[...] = jnp.zeros(u1buf.shape, BF16)
        _load_as_bf16(win_hbm, win_ref, stage_wide, stage_sems)
        _load_as_bf16(wup_hbm, wup_ref, stage_wide, stage_sems)
        _load_as_bf16(wco_hbm, wco_ref, stage_narrow, stage_sems)
        _load_as_bf16(wmo_hbm, wmo_ref, stage_narrow, stage_sems)
        _load_as_bf16(wao_hbm, wao_ref, stage_narrow, stage_sems)
        _load_as_bf16(wdn_hbm, wdn_ref, stage_narrow, stage_sems)

    @pl.when(mixer_seq_start)
    def _():
        zbuf[0:HALO, :] = jnp.zeros((HALO, D_MODEL), F32)
        kbuf[:, 0:BLOCK, :] = jnp.zeros((4, BLOCK, LANES), BF16)
        vbuf[:, 0:BLOCK] = jnp.zeros((KV_WIDTH, BLOCK), BF16)

    @pl.when(jnp.logical_or(ffn_seq_start, t == 0))
    def _():
        ubuf[0:HALO, :] = jnp.zeros((HALO, D_FF), F32)

    x_prev = x1buf[...]
    u_prev = u1buf[...]
    x = x_ref[0]
    u = _rmsnorm(x, nmw_ref[...]).astype(BF16)

    def proj(off, width):
        return _dot(u, win_ref[:, off:off + width]) + bin_ref[:, off:off + width]

    def ffn_up(c0, c1):
        up = _dot(u_prev, wup_ref[:, c0:c1])
        a = _causal_conv3(ubuf, up, fcw_ref, c0, c1) + fcb_ref[:, c0:c1]
        gate = _dot(u_prev, wup_ref[:, D_FF + c0:D_FF + c1])
        hbuf[:, c0:c1] = (jax.nn.silu(a) * gate).astype(BF16)

    ffn_up(*FF_CHUNKS[0])
    qkv = proj(OFF_Q, Q_WIDTH + 2 * KV_WIDTH)
    ffn_up(*FF_CHUNKS[1])

    ang = invf_ref[...] * pos_ref[0].astype(F32)
    cos_t, sin_t = jnp.cos(ang), jnp.sin(ang)
    cos = jnp.concatenate([cos_t] * 4, axis=0).T
    sin = jnp.concatenate([-sin_t, sin_t, -sin_t, sin_t], axis=0).T

    lane = lax.broadcasted_iota(jnp.int32, (tm, LANES), 1)
    first_half = (lane & (HEAD_DIM // 2)) == 0
    low_head = lane < HEAD_DIM

    def rope(v, c, s):
        partner = jnp.where(first_half, pltpu.roll(v, LANES - HEAD_DIM // 2, 1),
                            pltpu.roll(v, HEAD_DIM // 2, 1))
        return v * c + partner * s

    scale = HEAD_DIM ** -0.5
    cos_q, sin_q = cos * scale, sin * scale
    for c in range(Q_WIDTH // LANES):
        sl = slice(c * LANES, (c + 1) * LANES)
        qbuf[:, sl] = rope(qkv[:, sl], cos_q, sin_q).astype(BF16)

    k = rope(qkv[:, Q_WIDTH:Q_WIDTH + KV_WIDTH], cos, sin)
    krot = pltpu.roll(k, HEAD_DIM, 1)
    kzero = jnp.zeros_like(k)
    rows = slice(BLOCK, BLOCK + tm)
    kbuf[0, rows, :] = jnp.where(low_head, k, kzero).astype(BF16)
    kbuf[1, rows, :] = jnp.where(low_head, kzero, krot).astype(BF16)
    kbuf[2, rows, :] = jnp.where(low_head, krot, kzero).astype(BF16)
    kbuf[3, rows, :] = jnp.where(low_head, kzero, k).astype(BF16)
    vbuf[:, BLOCK:BLOCK + tm] = qkv[:, Q_WIDTH + KV_WIDTH:].T.astype(BF16)

    key2 = lax.broadcasted_iota(jnp.int32, (2 * BLOCK, 2 * BLOCK), 0)
    qry2 = lax.broadcasted_iota(jnp.int32, (2 * BLOCK, 2 * BLOCK), 1) & (BLOCK - 1)
    band = (key2 > qry2) & (key2 <= qry2 + BLOCK)
    neg_inf = jnp.full((2 * BLOCK, 2 * BLOCK), -jnp.inf, F32)
    zeros2 = jnp.zeros((2 * BLOCK, 2 * BLOCK), F32)
    bias = jnp.where(band, zeros2, neg_inf)
    bias_seq_start = jnp.where(band & (key2 >= BLOCK), zeros2, neg_inf)
    bias_first = jnp.where(mixer_seq_start, bias_seq_start, bias)

    z = proj(OFF_CC, D_MODEL) * proj(OFF_CX, D_MODEL)

    blocks = [(i, g) for i in range(tm // BLOCK) for g in range(N_KV_HEADS)]
    scores = {}
    for i, g in blocks:
        qrows = slice(i * BLOCK, (i + 1) * BLOCK)
        krows = slice(i * BLOCK, (i + 2) * BLOCK)
        blk_bias = bias_first if i == 0 else bias
        c0 = g * 2 * LANES
        qpair = jnp.concatenate([qbuf[qrows, c0:c0 + LANES],
                                 qbuf[qrows, c0 + LANES:c0 + 2 * LANES]], axis=0)
        scores[i, g] = (_dot_nt(kbuf[2 * g, krows, :], qpair) + blk_bias,
                        _dot_nt(kbuf[2 * g + 1, krows, :], qpair) + blk_bias)

    ffn_up(*FF_CHUNKS[2])

    top_pair = lax.broadcasted_iota(jnp.int32, (1, 2 * BLOCK), 1) < BLOCK
    sum_row = lax.broadcasted_iota(jnp.int32, (2 * HALO, 4 * BLOCK), 0)
    sum_col = lax.broadcasted_iota(jnp.int32, (2 * HALO, 4 * BLOCK), 1)
    sum_rows = jnp.where(sum_row == (sum_col >= 2 * BLOCK).astype(jnp.int32), 1.0, 0.0)
    sum_rows = sum_rows.astype(BF16)
    vzero = jnp.zeros((HEAD_DIM, 2 * BLOCK), BF16)

    for i, g in blocks:
        qrows = slice(i * BLOCK, (i + 1) * BLOCK)
        kcols = slice(i * BLOCK, (i + 2) * BLOCK)
        c0 = g * 2 * LANES
        s_even, s_odd = scores[i, g]
        sink_even = jnp.where(top_pair, sinks_ref[4 * g], sinks_ref[4 * g + 2])
        sink_odd = jnp.where(top_pair, sinks_ref[4 * g + 1], sinks_ref[4 * g + 3])
        m_even = jnp.maximum(jnp.max(s_even, axis=0, keepdims=True), sink_even)
        m_odd = jnp.maximum(jnp.max(s_odd, axis=0, keepdims=True), sink_odd)
        p_t = jnp.concatenate([jnp.exp(s_even - m_even).astype(BF16),
                               jnp.exp(s_odd - m_odd).astype(BF16)], axis=0)
        v_t = vbuf[g * HEAD_DIM:(g + 1) * HEAD_DIM, kcols]
        lhs = jnp.concatenate([jnp.concatenate([v_t, vzero], axis=1),
                               jnp.concatenate([vzero, v_t], axis=1), sum_rows], axis=0)
        r = _dot(lhs, p_t)
        inv_even = 1.0 / (r[2 * HEAD_DIM:2 * HEAD_DIM + 1] + jnp.exp(sink_even - m_even))
        inv_odd = 1.0 / (r[2 * HEAD_DIM + 1:2 * HEAD_DIM + 2] + jnp.exp(sink_odd - m_odd))
        out = jnp.concatenate([r[0:HEAD_DIM] * inv_even,
                               r[HEAD_DIM:2 * HEAD_DIM] * inv_odd], axis=0).T.astype(BF16)
        abuf[qrows, c0:c0 + LANES] = out[:BLOCK]
        abuf[qrows, c0 + LANES:c0 + 2 * LANES] = out[BLOCK:]

    kbuf[:, 0:BLOCK, :] = kbuf[:, tm:tm + BLOCK, :]
    vbuf[:, 0:BLOCK] = vbuf[:, tm:tm + BLOCK]

    conv = _causal_conv3(zbuf, z, cmw_ref, 0, D_MODEL)
    gated = (proj(OFF_CB, D_MODEL) * conv).astype(BF16)
    gate_conv = proj(OFF_GC, D_MODEL)
    gate_attn = proj(OFF_GA, D_MODEL)
    y_conv = _dot(gated, wco_ref[...])

    y_attn = _dot(abuf[...], wao_ref[...]) + bao_ref[...]

    y = x_prev + _dot(hbuf[...], wdn_ref[...])
    o_ref[0] = _rmsnorm(y, fin_ref[...]) if final_norm else y

    merged = jax.nn.sigmoid(gate_conv) * y_conv + jax.nn.sigmoid(gate_attn) * y_attn
    x1 = x + _dot(merged.astype(BF16), wmo_ref[...])
    x1buf[...] = x1
    u1buf[...] = _rmsnorm(x1, nfw_ref[...]).astype(BF16)


def _whole(space=pltpu.VMEM):
    return pl.BlockSpec(memory_space=space)


def kernel(x, positions, norm_mix_w, w_in, b_in, conv_mix_w, w_conv_out, w_attn_out, b_attn_out,
           sinks, w_mix_out, norm_ffn_w, w_ffn_up, ffn_conv_w, ffn_conv_b, w_ffn_down, norm_final_w):
    b, s, d = x.shape
    depth = w_in.shape[0]
    tm = TOKEN_TILE
    assert d == D_MODEL and s % tm == 0 and tm % BLOCK == 0
    nj = s // tm
    n_tiles = b * nj

    def in_tile(t):
        return jnp.minimum(t, n_tiles - 1)

    x_spec = pl.BlockSpec((1, tm, D_MODEL), lambda t: (in_tile(t), 0, 0))
    pos_spec = pl.BlockSpec((1, 1, tm), lambda t: (in_tile(t), 0, 0))
    out_spec = pl.BlockSpec((1, tm, D_MODEL), lambda t: (jnp.maximum(t - 1, 0), 0, 0))

    inv_freq = (ROPE_THETA ** (-jnp.arange(0, HEAD_DIM, 2, dtype=F32) / HEAD_DIM)).reshape(-1, 1)
    pos_tiles = positions.reshape(n_tiles, 1, tm)

    def rows(v):
        v = v.reshape(-1, v.shape[-1])
        return jnp.pad(v, ((0, 0), (0, IN_WIDTH - v.shape[-1])))

    def layer(final_norm):
        return pl.pallas_call(
            functools.partial(_block_kernel, nj, final_norm),
            grid=(n_tiles + 1,),
            in_specs=([_whole(pltpu.SMEM), pos_spec, x_spec] + [_whole()] * 2
                      + [pl.BlockSpec(memory_space=pl.ANY)] * 6),
            out_specs=out_spec,
            out_shape=jax.ShapeDtypeStruct((n_tiles, tm, D_MODEL), F32),
            scratch_shapes=[
                pltpu.VMEM((HALO, D_MODEL), F32),
                pltpu.VMEM((4, BLOCK + tm, LANES), BF16),
                pltpu.VMEM((KV_WIDTH, BLOCK + tm), BF16),
                pltpu.VMEM((tm, Q_WIDTH), BF16),
                pltpu.VMEM((tm, Q_WIDTH), BF16),
                pltpu.VMEM((tm, D_MODEL), F32),
                pltpu.VMEM((tm, D_MODEL), BF16),
                pltpu.VMEM((HALO, D_FF), F32),
                pltpu.VMEM((tm, D_FF), BF16),
                pltpu.VMEM((D_MODEL, IN_WIDTH), BF16),
                pltpu.VMEM((D_MODEL, 2 * D_FF), BF16),
                pltpu.VMEM((D_MODEL, D_MODEL), BF16),
                pltpu.VMEM((D_MODEL, D_MODEL), BF16),
                pltpu.VMEM((Q_WIDTH, D_MODEL), BF16),
                pltpu.VMEM((D_FF, D_MODEL), BF16),
                pltpu.VMEM((STAGE_SLOTS, STAGE_ROWS_WIDE, IN_WIDTH), F32),
                pltpu.VMEM((STAGE_SLOTS, STAGE_ROWS_NARROW, D_MODEL), F32),
                pltpu.SemaphoreType.DMA((STAGE_SLOTS,)),
            ],
            compiler_params=pltpu.CompilerParams(dimension_semantics=("arbitrary",),
                                                 vmem_limit_bytes=VMEM_LIMIT_BYTES),
            name="decoder_block",
        )

    x = x.reshape(n_tiles, tm, D_MODEL)
    for l in range(depth):
        params = jnp.concatenate(
            [rows(b_in[l]), rows(norm_mix_w[l]), rows(b_attn_out[l]), rows(norm_ffn_w[l]),
             rows(norm_final_w), rows(conv_mix_w[l]), rows(ffn_conv_w[l]), rows(ffn_conv_b[l])],
            axis=0)
        x = layer(l == depth - 1)(
            sinks[l], pos_tiles, x, inv_freq, params, w_in[l], w_ffn_up[l], w_conv_out[l],
            w_mix_out[l], w_attn_out[l], w_ffn_down[l])
    return x.reshape(b, s, D_MODEL)
```

```python
import functools

import jax
import jax.numpy as jnp
from jax import lax
from jax.experimental import pallas as pl
from jax.experimental.pallas import tpu as pltpu

D_MODEL = 1024
HEAD_DIM = 64
N_Q_HEADS = 8
N_KV_HEADS = 2
BLOCK = 128
ROPE_THETA = 10000.0
Q_WIDTH = N_Q_HEADS * HEAD_DIM
KV_WIDTH = N_KV_HEADS * HEAD_DIM
D_FF = 2816
EPS = 1e-5

OFF_CB = 0
OFF_CC = OFF_CB + D_MODEL
OFF_CX = OFF_CC + D_MODEL
OFF_Q = OFF_CX + D_MODEL
OFF_K = OFF_Q + Q_WIDTH
OFF_V = OFF_K + KV_WIDTH
OFF_GC = OFF_V + KV_WIDTH
OFF_GA = OFF_GC + D_MODEL
IN_WIDTH = OFF_GA + D_MODEL

ROW_B_IN, ROW_NORM_MIX, ROW_B_ATTN_OUT, ROW_NORM_FFN, ROW_NORM_FINAL = 0, 1, 2, 3, 4
ROW_CONV_MIX, ROW_FFN_CONV, ROW_FFN_CONV_B = 5, 8, 11

LANES = 128
HALO = 8
TOKEN_TILE = 256
FF_CHUNKS = ((0, 1024), (1024, 2048), (2048, D_FF))
VMEM_LIMIT_BYTES = 60 * 1024 * 1024

STAGE_SLOTS = 4
STAGE_ROWS_WIDE = 32
STAGE_ROWS_NARROW = 128

F32 = jnp.float32
BF16 = jnp.bfloat16


def _rmsnorm(x, w):
    ms = jnp.mean(x * x, axis=-1, keepdims=True)
    return x * lax.rsqrt(ms + EPS) * w


def _sigmoid(x):
    return 0.5 * jnp.tanh(0.5 * x) + 0.5


def _dot(a, b):
    return jnp.dot(a, b, preferred_element_type=F32)


def _dot_nt(a, b):
    return lax.dot_general(a, b, (((1,), (1,)), ((), ())), preferred_element_type=F32)


def _causal_conv3(halo_ref, cur, taps_ref, c0, c1):
    tm = cur.shape[0]
    first_row = lax.broadcasted_iota(jnp.int32, (HALO, c1 - c0), 0) == 0

    def shift_down(v, halo_row):
        rolled = pltpu.roll(v, 1, 0)
        head = jnp.where(first_row, halo_row, rolled[0:HALO])
        return jnp.concatenate([head, rolled[HALO:]], axis=0)

    prev1 = shift_down(cur, halo_ref[HALO - 1:HALO, c0:c1])
    prev2 = shift_down(prev1, halo_ref[HALO - 2:HALO - 1, c0:c1])
    halo_ref[HALO - 2:HALO, c0:c1] = cur[tm - 2:tm]
    return (taps_ref[0:1, c0:c1] * prev2 + taps_ref[1:2, c0:c1] * prev1
            + taps_ref[2:3, c0:c1] * cur)


def _load_as_bf16(src_hbm, dst_ref, stage, sems):
    rows, width = src_hbm.shape
    chunk = stage.shape[1]
    assert rows % chunk == 0 and width <= stage.shape[2]
    n = rows // chunk
    ahead = STAGE_SLOTS - 1

    def copy(i, slot):
        r0 = i * chunk if isinstance(i, int) else pl.multiple_of(i * chunk, chunk)
        return pltpu.make_async_copy(src_hbm.at[pl.ds(r0, chunk), :],
                                     stage.at[slot, :, 0:width], sems.at[slot])

    for i in range(min(ahead, n)):
        copy(i, i).start()

    def body(i, carry):
        slot = lax.rem(i, STAGE_SLOTS)
        copy(i, slot).wait()

        @pl.when(i + ahead < n)
        def _():
            copy(i + ahead, lax.rem(i + ahead, STAGE_SLOTS)).start()

        r0 = pl.multiple_of(i * chunk, chunk)
        dst_ref[pl.ds(r0, chunk), :] = stage[slot, :, 0:width].astype(BF16)
        return carry

    lax.fori_loop(0, n, body, 0)


def _block_kernel(tiles_per_seq, final_norm,
                  sinks_ref, pos_ref, x_ref, xnext_ref, invf_ref, prm_ref, win_hbm, wup_hbm,
                  wco_hbm, wmo_hbm, wao_hbm, wdn_hbm, o_ref,
                  zbuf, kbuf, vbuf, qbuf, abuf, x1buf, u1buf, umbuf, qkvbuf, ubuf, hbuf,
                  win_ref, wup_ref, wco_ref, wmo_ref, wao_ref, wdn_ref,
                  stage_wide, stage_narrow, stage_sems):
    t = pl.program_id(0)
    tm = x_ref.shape[1]
    mixer_seq_start = lax.rem(t, tiles_per_seq) == 0
    ffn_seq_start = lax.rem(t + tiles_per_seq - 1, tiles_per_seq) == 0

    bin_ref = prm_ref.at[ROW_B_IN:ROW_B_IN + 1, :]
    nmw_ref = prm_ref.at[ROW_NORM_MIX:ROW_NORM_MIX + 1, 0:D_MODEL]
    bao_ref = prm_ref.at[ROW_B_ATTN_OUT:ROW_B_ATTN_OUT + 1, 0:D_MODEL]
    nfw_ref = prm_ref.at[ROW_NORM_FFN:ROW_NORM_FFN + 1, 0:D_MODEL]
    fin_ref = prm_ref.at[ROW_NORM_FINAL:ROW_NORM_FINAL + 1, 0:D_MODEL]
    cmw_ref = prm_ref.at[ROW_CONV_MIX:ROW_CONV_MIX + 3, 0:D_MODEL]
    fcw_ref = prm_ref.at[ROW_FFN_CONV:ROW_FFN_CONV + 3, 0:D_FF]
    fcb_ref = prm_ref.at[ROW_FFN_CONV_B:ROW_FFN_CONV_B + 1, 0:D_FF]

    @pl.when(t == 0)
    def _():
        x1buf[...] = jnp.zeros(x1buf.shape, F32)
        u1buf[...] = jnp.zeros(u1buf.shape, BF16)
        _load_as_bf16(win_hbm, win_ref, stage_wide, stage_sems)
        _load_as_bf16(wup_hbm, wup_ref, stage_wide, stage_sems)
        _load_as_bf16(wco_hbm, wco_ref, stage_narrow, stage_sems)
        _load_as_bf16(wmo_hbm, wmo_ref, stage_narrow, stage_sems)
        _load_as_bf16(wao_hbm, wao_ref, stage_narrow, stage_sems)
        _load_as_bf16(wdn_hbm, wdn_ref, stage_narrow, stage_sems)
        u_first = _rmsnorm(x_ref[0], nmw_ref[...]).astype(BF16)
        umbuf[...] = u_first
        qkvbuf[...] = (_dot(u_first, win_ref[:, OFF_Q:OFF_GC]) + bin_ref[:, OFF_Q:OFF_GC])

    @pl.when(mixer_seq_start)
    def _():
        zbuf[0:HALO, :] = jnp.zeros((HALO, D_MODEL), F32)
        kbuf[:, 0:BLOCK, :] = jnp.zeros((4, BLOCK, LANES), BF16)
        vbuf[:, 0:BLOCK] = jnp.zeros((KV_WIDTH, BLOCK), BF16)

    @pl.when(jnp.logical_or(ffn_seq_start, t == 0))
    def _():
        ubuf[0:HALO, :] = jnp.zeros((HALO, D_FF), F32)

    x_prev = x1buf[...]
    u_prev = u1buf[...]
    x = x_ref[0]
    u = umbuf[...]

    def proj_of(lhs, off, width):
        return _dot(lhs, win_ref[:, off:off + width]) + bin_ref[:, off:off + width]

    def proj(off, width):
        return proj_of(u, off, width)

    def ffn_up(c0, c1):
        up = _dot(u_prev, wup_ref[:, c0:c1])
        a = _causal_conv3(ubuf, up, fcw_ref, c0, c1) + fcb_ref[:, c0:c1]
        gate = _dot(u_prev, wup_ref[:, D_FF + c0:D_FF + c1])
        hbuf[:, c0:c1] = (a * _sigmoid(a) * gate).astype(BF16)

    qkv = qkvbuf[...]
    ffn_up(*FF_CHUNKS[0])
    ffn_up(*FF_CHUNKS[1])

    ang = invf_ref[...] * pos_ref[0].astype(F32)
    cos_t, sin_t = jnp.cos(ang), jnp.sin(ang)
    cos = jnp.concatenate([cos_t] * 4, axis=0).T
    sin = jnp.concatenate([-sin_t, sin_t, -sin_t, sin_t], axis=0).T

    lane = lax.broadcasted_iota(jnp.int32, (tm, LANES), 1)
    first_half = (lane & (HEAD_DIM // 2)) == 0
    low_head = lane < HEAD_DIM

    def rope(v, c, s):
        partner = jnp.where(first_half, pltpu.roll(v, LANES - HEAD_DIM // 2, 1),
                            pltpu.roll(v, HEAD_DIM // 2, 1))
        return v * c + partner * s

    scale = HEAD_DIM ** -0.5
    cos_q, sin_q = cos * scale, sin * scale
    for c in range(Q_WIDTH // LANES):
        sl = slice(c * LANES, (c + 1) * LANES)
        qbuf[:, sl] = rope(qkv[:, sl], cos_q, sin_q).astype(BF16)

    k = rope(qkv[:, Q_WIDTH:Q_WIDTH + KV_WIDTH], cos, sin)
    krot = pltpu.roll(k, HEAD_DIM, 1)
    kzero = jnp.zeros_like(k)
    rows = slice(BLOCK, BLOCK + tm)
    kbuf[0, rows, :] = jnp.where(low_head, k, kzero).astype(BF16)
    kbuf[1, rows, :] = jnp.where(low_head, kzero, krot).astype(BF16)
    kbuf[2, rows, :] = jnp.where(low_head, krot, kzero).astype(BF16)
    kbuf[3, rows, :] = jnp.where(low_head, kzero, k).astype(BF16)
    vbuf[:, BLOCK:BLOCK + tm] = qkv[:, Q_WIDTH + KV_WIDTH:].T.astype(BF16)

    key2 = lax.broadcasted_iota(jnp.int32, (2 * BLOCK, 2 * BLOCK), 0)
    qry2 = lax.broadcasted_iota(jnp.int32, (2 * BLOCK, 2 * BLOCK), 1) & (BLOCK - 1)
    band = (key2 > qry2) & (key2 <= qry2 + BLOCK)
    neg_inf = jnp.full((2 * BLOCK, 2 * BLOCK), -jnp.inf, F32)
    zeros2 = jnp.zeros((2 * BLOCK, 2 * BLOCK), F32)
    bias = jnp.where(band, zeros2, neg_inf)
    bias_seq_start = jnp.where(band & (key2 >= BLOCK), zeros2, neg_inf)
    bias_first = jnp.where(mixer_seq_start, bias_seq_start, bias)

    z = proj(OFF_CC, D_MODEL) * proj(OFF_CX, D_MODEL)

    blocks = [(i, g) for i in range(tm // BLOCK) for g in range(N_KV_HEADS)]
    scores = {}
    for i, g in blocks:
        qrows = slice(i * BLOCK, (i + 1) * BLOCK)
        krows = slice(i * BLOCK, (i + 2) * BLOCK)
        blk_bias = bias_first if i == 0 else bias
        c0 = g * 2 * LANES
        qpair = jnp.concatenate([qbuf[qrows, c0:c0 + LANES],
                                 qbuf[qrows, c0 + LANES:c0 + 2 * LANES]], axis=0)
        scores[i, g] = (_dot_nt(kbuf[2 * g, krows, :], qpair) + blk_bias,
                        _dot_nt(kbuf[2 * g + 1, krows, :], qpair) + blk_bias)

    ffn_up(*FF_CHUNKS[2])

    top_pair = lax.broadcasted_iota(jnp.int32, (1, 2 * BLOCK), 1) < BLOCK
    sum_row = lax.broadcasted_iota(jnp.int32, (2 * HALO, 4 * BLOCK), 0)
    sum_col = lax.broadcasted_iota(jnp.int32, (2 * HALO, 4 * BLOCK), 1)
    sum_rows = jnp.where(sum_row == (sum_col >= 2 * BLOCK).astype(jnp.int32), 1.0, 0.0)
    sum_rows = sum_rows.astype(BF16)
    vzero = jnp.zeros((HEAD_DIM, 2 * BLOCK), BF16)

    for i, g in blocks:
        qrows = slice(i * BLOCK, (i + 1) * BLOCK)
        kcols = slice(i * BLOCK, (i + 2) * BLOCK)
        c0 = g * 2 * LANES
        s_even, s_odd = scores[i, g]
        sink_even = jnp.where(top_pair, sinks_ref[4 * g], sinks_ref[4 * g + 2])
        sink_odd = jnp.where(top_pair, sinks_ref[4 * g + 1], sinks_ref[4 * g + 3])
        m_even = jnp.maximum(jnp.max(s_even, axis=0, keepdims=True), sink_even)
        m_odd = jnp.maximum(jnp.max(s_odd, axis=0, keepdims=True), sink_odd)
        p_t = jnp.concatenate([jnp.exp(s_even - m_even).astype(BF16),
                               jnp.exp(s_odd - m_odd).astype(BF16)], axis=0)
        v_t = vbuf[g * HEAD_DIM:(g + 1) * HEAD_DIM, kcols]
        lhs = jnp.concatenate([jnp.concatenate([v_t, vzero], axis=1),
                               jnp.concatenate([vzero, v_t], axis=1), sum_rows], axis=0)
        r = _dot(lhs, p_t)
        inv_even = 1.0 / (r[2 * HEAD_DIM:2 * HEAD_DIM + 1] + jnp.exp(sink_even - m_even))
        inv_odd = 1.0 / (r[2 * HEAD_DIM + 1:2 * HEAD_DIM + 2] + jnp.exp(sink_odd - m_odd))
        out = jnp.concatenate([r[0:HEAD_DIM] * inv_even,
                               r[HEAD_DIM:2 * HEAD_DIM] * inv_odd], axis=0).T.astype(BF16)
        abuf[qrows, c0:c0 + LANES] = out[:BLOCK]
        abuf[qrows, c0 + LANES:c0 + 2 * LANES] = out[BLOCK:]

    kbuf[:, 0:BLOCK, :] = kbuf[:, tm:tm + BLOCK, :]
    vbuf[:, 0:BLOCK] = vbuf[:, tm:tm + BLOCK]

    conv = _causal_conv3(zbuf, z, cmw_ref, 0, D_MODEL)
    gated = (proj(OFF_CB, D_MODEL) * conv).astype(BF16)
    gate_conv = proj(OFF_GC, D_MODEL)
    gate_attn = proj(OFF_GA, D_MODEL)
    y_conv = _dot(gated, wco_ref[...])

    y_attn = _dot(abuf[...], wao_ref[...]) + bao_ref[...]

    y = x_prev + _dot(hbuf[...], wdn_ref[...])
    o_ref[0] = _rmsnorm(y, fin_ref[...]) if final_norm else y

    merged = _sigmoid(gate_conv) * y_conv + _sigmoid(gate_attn) * y_attn
    x1 = x + _dot(merged.astype(BF16), wmo_ref[...])
    x1buf[...] = x1
    u1buf[...] = _rmsnorm(x1, nfw_ref[...]).astype(BF16)

    u_next = _rmsnorm(xnext_ref[0], nmw_ref[...]).astype(BF16)
    umbuf[...] = u_next
    qkvbuf[...] = proj_of(u_next, OFF_Q, Q_WIDTH + 2 * KV_WIDTH)


def _whole(space=pltpu.VMEM):
    return pl.BlockSpec(memory_space=space)


def kernel(x, positions, norm_mix_w, w_in, b_in, conv_mix_w, w_conv_out, w_attn_out, b_attn_out,
           sinks, w_mix_out, norm_ffn_w, w_ffn_up, ffn_conv_w, ffn_conv_b, w_ffn_down, norm_final_w):
    b, s, d = x.shape
    depth = w_in.shape[0]
    tm = TOKEN_TILE
    assert d == D_MODEL and s % tm == 0 and tm % BLOCK == 0
    nj = s // tm
    n_tiles = b * nj

    def in_tile(t):
        return jnp.minimum(t, n_tiles - 1)

    x_spec = pl.BlockSpec((1, tm, D_MODEL), lambda t: (in_tile(t), 0, 0))
    xnext_spec = pl.BlockSpec((1, tm, D_MODEL), lambda t: (in_tile(t + 1), 0, 0))
    pos_spec = pl.BlockSpec((1, 1, tm), lambda t: (in_tile(t), 0, 0))
    out_spec = pl.BlockSpec((1, tm, D_MODEL), lambda t: (jnp.maximum(t - 1, 0), 0, 0))

    inv_freq = (ROPE_THETA ** (-jnp.arange(0, HEAD_DIM, 2, dtype=F32) / HEAD_DIM)).reshape(-1, 1)
    pos_tiles = positions.reshape(n_tiles, 1, tm)

    def rows(v):
        v = v.reshape(-1, v.shape[-1])
        return jnp.pad(v, ((0, 0), (0, IN_WIDTH - v.shape[-1])))

    def layer(final_norm):
        return pl.pallas_call(
            functools.partial(_block_kernel, nj, final_norm),
            grid=(n_tiles + 1,),
            in_specs=([_whole(pltpu.SMEM), pos_spec, x_spec, xnext_spec] + [_whole()] * 2
                      + [pl.BlockSpec(memory_space=pl.ANY)] * 6),
            out_specs=out_spec,
            out_shape=jax.ShapeDtypeStruct((n_tiles, tm, D_MODEL), F32),
            scratch_shapes=[
                pltpu.VMEM((HALO, D_MODEL), F32),
                pltpu.VMEM((4, BLOCK + tm, LANES), BF16),
                pltpu.VMEM((KV_WIDTH, BLOCK + tm), BF16),
                pltpu.VMEM((tm, Q_WIDTH), BF16),
                pltpu.VMEM((tm, Q_WIDTH), BF16),
                pltpu.VMEM((tm, D_MODEL), F32),
                pltpu.VMEM((tm, D_MODEL), BF16),
                pltpu.VMEM((tm, D_MODEL), BF16),
                pltpu.VMEM((tm, Q_WIDTH + 2 * KV_WIDTH), F32),
                pltpu.VMEM((HALO, D_FF), F32),
                pltpu.VMEM((tm, D_FF), BF16),
                pltpu.VMEM((D_MODEL, IN_WIDTH), BF16),
                pltpu.VMEM((D_MODEL, 2 * D_FF), BF16),
                pltpu.VMEM((D_MODEL, D_MODEL), BF16),
                pltpu.VMEM((D_MODEL, D_MODEL), BF16),
                pltpu.VMEM((Q_WIDTH, D_MODEL), BF16),
                pltpu.VMEM((D_FF, D_MODEL), BF16),
                pltpu.VMEM((STAGE_SLOTS, STAGE_ROWS_WIDE, IN_WIDTH), F32),
                pltpu.VMEM((STAGE_SLOTS, STAGE_ROWS_NARROW, D_MODEL), F32),
                pltpu.SemaphoreType.DMA((STAGE_SLOTS,)),
            ],
            compiler_params=pltpu.CompilerParams(dimension_semantics=("arbitrary",),
                                                 vmem_limit_bytes=VMEM_LIMIT_BYTES),
            name="decoder_block",
        )

    x = x.reshape(n_tiles, tm, D_MODEL)
    for l in range(depth):
        params = jnp.concatenate(
            [rows(b_in[l]), rows(norm_mix_w[l]), rows(b_attn_out[l]), rows(norm_ffn_w[l]),
             rows(norm_final_w), rows(conv_mix_w[l]), rows(ffn_conv_w[l]), rows(ffn_conv_b[l])],
            axis=0)
        x = layer(l == depth - 1)(
            sinks[l], pos_tiles, x, x, inv_freq, params, w_in[l], w_ffn_up[l], w_conv_out[l],
            w_mix_out[l], w_attn_out[l], w_ffn_down[l])
    return x.reshape(b, s, D_MODEL)
```

```python
import functools

import jax
import jax.numpy as jnp
from jax import lax
from jax.experimental import pallas as pl
from jax.experimental.pallas import tpu as pltpu

D_MODEL = 1024
HEAD_DIM = 64
N_Q_HEADS = 8
N_KV_HEADS = 2
BLOCK = 128
ROPE_THETA = 10000.0
Q_WIDTH = N_Q_HEADS * HEAD_DIM
KV_WIDTH = N_KV_HEADS * HEAD_DIM
D_FF = 2816
EPS = 1e-5

OFF_CB = 0
OFF_CC = OFF_CB + D_MODEL
OFF_CX = OFF_CC + D_MODEL
OFF_Q = OFF_CX + D_MODEL
OFF_K = OFF_Q + Q_WIDTH
OFF_V = OFF_K + KV_WIDTH
OFF_GC = OFF_V + KV_WIDTH
OFF_GA = OFF_GC + D_MODEL
IN_WIDTH = OFF_GA + D_MODEL

ROW_B_IN, ROW_NORM_MIX, ROW_B_ATTN_OUT, ROW_NORM_FFN, ROW_NORM_FINAL = 0, 1, 2, 3, 4
ROW_CONV_MIX, ROW_FFN_CONV, ROW_FFN_CONV_B = 5, 8, 11

LANES = 128
HALO = 8
TOKEN_TILE = 256
FF_CHUNKS = ((0, 1024), (1024, 2048), (2048, D_FF))
VMEM_LIMIT_BYTES = 60 * 1024 * 1024

STAGE_SLOTS = 4
STAGE_ROWS_WIDE = 32
STAGE_ROWS_NARROW = 128

F32 = jnp.float32
BF16 = jnp.bfloat16


def _rmsnorm(x, w):
    ms = jnp.mean(x * x, axis=-1, keepdims=True)
    return x * lax.rsqrt(ms + EPS) * w


def _dot(a, b):
    return jnp.dot(a, b, preferred_element_type=F32)


def _dot_nt(a, b):
    return lax.dot_general(a, b, (((1,), (1,)), ((), ())), preferred_element_type=F32)


def _causal_conv3(halo_ref, cur, taps_ref, c0, c1):
    tm = cur.shape[0]
    first_row = lax.broadcasted_iota(jnp.int32, (HALO, c1 - c0), 0) == 0

    def shift_down(v, halo_row):
        rolled = pltpu.roll(v, 1, 0)
        head = jnp.where(first_row, halo_row, rolled[0:HALO])
        return jnp.concatenate([head, rolled[HALO:]], axis=0)

    prev1 = shift_down(cur, halo_ref[HALO - 1:HALO, c0:c1])
    prev2 = shift_down(prev1, halo_ref[HALO - 2:HALO - 1, c0:c1])
    halo_ref[HALO - 2:HALO, c0:c1] = cur[tm - 2:tm]
    return (taps_ref[0:1, c0:c1] * prev2 + taps_ref[1:2, c0:c1] * prev1
            + taps_ref[2:3, c0:c1] * cur)


def _load_as_bf16(src_hbm, dst_ref, stage, sems):
    rows, width = src_hbm.shape
    chunk = stage.shape[1]
    assert rows % chunk == 0 and width <= stage.shape[2]
    n = rows // chunk
    ahead = STAGE_SLOTS - 1

    def copy(i, slot):
        r0 = i * chunk if isinstance(i, int) else pl.multiple_of(i * chunk, chunk)
        return pltpu.make_async_copy(src_hbm.at[pl.ds(r0, chunk), :],
                                     stage.at[slot, :, 0:width], sems.at[slot])

    for i in range(min(ahead, n)):
        copy(i, i).start()

    def body(i, carry):
        slot = lax.rem(i, STAGE_SLOTS)
        copy(i, slot).wait()

        @pl.when(i + ahead < n)
        def _():
            copy(i + ahead, lax.rem(i + ahead, STAGE_SLOTS)).start()

        r0 = pl.multiple_of(i * chunk, chunk)
        dst_ref[pl.ds(r0, chunk), :] = stage[slot, :, 0:width].astype(BF16)
        return carry

    lax.fori_loop(0, n, body, 0)


def _block_kernel(tiles_per_seq, final_norm,
                  sinks_ref, pos_ref, x_ref, xnext_ref, invf_ref, prm_ref, win_hbm, wup_hbm,
                  wco_hbm, wmo_hbm, wao_hbm, wdn_hbm, o_ref,
                  zbuf, kbuf, vbuf, qbuf, abuf, x1buf, u1buf, umbuf, qkvbuf, ubuf, hbuf,
                  win_ref, wup_ref, wco_ref, wmo_ref, wao_ref, wdn_ref,
                  stage_wide, stage_narrow, stage_sems):
    t = pl.program_id(0)
    tm = x_ref.shape[1]
    mixer_seq_start = lax.rem(t, tiles_per_seq) == 0
    ffn_seq_start = lax.rem(t + tiles_per_seq - 1, tiles_per_seq) == 0

    bin_ref = prm_ref.at[ROW_B_IN:ROW_B_IN + 1, :]
    nmw_ref = prm_ref.at[ROW_NORM_MIX:ROW_NORM_MIX + 1, 0:D_MODEL]
    bao_ref = prm_ref.at[ROW_B_ATTN_OUT:ROW_B_ATTN_OUT + 1, 0:D_MODEL]
    nfw_ref = prm_ref.at[ROW_NORM_FFN:ROW_NORM_FFN + 1, 0:D_MODEL]
    fin_ref = prm_ref.at[ROW_NORM_FINAL:ROW_NORM_FINAL + 1, 0:D_MODEL]
    cmw_ref = prm_ref.at[ROW_CONV_MIX:ROW_CONV_MIX + 3, 0:D_MODEL]
    fcw_ref = prm_ref.at[ROW_FFN_CONV:ROW_FFN_CONV + 3, 0:D_FF]
    fcb_ref = prm_ref.at[ROW_FFN_CONV_B:ROW_FFN_CONV_B + 1, 0:D_FF]

    @pl.when(t == 0)
    def _():
        x1buf[...] = jnp.zeros(x1buf.shape, F32)
        u1buf[...] = jnp.zeros(u1buf.shape, BF16)
        _load_as_bf16(win_hbm, win_ref, stage_wide, stage_sems)
        _load_as_bf16(wup_hbm, wup_ref, stage_wide, stage_sems)
        _load_as_bf16(wco_hbm, wco_ref, stage_narrow, stage_sems)
        _load_as_bf16(wmo_hbm, wmo_ref, stage_narrow, stage_sems)
        _load_as_bf16(wao_hbm, wao_ref, stage_narrow, stage_sems)
        _load_as_bf16(wdn_hbm, wdn_ref, stage_narrow, stage_sems)
        u_first = _rmsnorm(x_ref[0], nmw_ref[...]).astype(BF16)
        umbuf[...] = u_first
        qkvbuf[...] = (_dot(u_first, win_ref[:, OFF_Q:OFF_GC]) + bin_ref[:, OFF_Q:OFF_GC])

    @pl.when(mixer_seq_start)
    def _():
        zbuf[0:HALO, :] = jnp.zeros((HALO, D_MODEL), F32)
        kbuf[:, 0:BLOCK, :] = jnp.zeros((4, BLOCK, LANES), BF16)
        vbuf[:, 0:BLOCK] = jnp.zeros((KV_WIDTH, BLOCK), BF16)

    @pl.when(jnp.logical_or(ffn_seq_start, t == 0))
    def _():
        ubuf[0:HALO, :] = jnp.zeros((HALO, D_FF), F32)

    x_prev = x1buf[...]
    u_prev = u1buf[...]
    x = x_ref[0]
    u = umbuf[...]

    def proj_of(lhs, off, width):
        return _dot(lhs, win_ref[:, off:off + width]) + bin_ref[:, off:off + width]

    def proj(off, width):
        return proj_of(u, off, width)

    def ffn_up(c0, c1):
        up = _dot(u_prev, wup_ref[:, c0:c1])
        a = _causal_conv3(ubuf, up, fcw_ref, c0, c1) + fcb_ref[:, c0:c1]
        gate = _dot(u_prev, wup_ref[:, D_FF + c0:D_FF + c1])
        hbuf[:, c0:c1] = (jax.nn.silu(a) * gate).astype(BF16)

    qkv = qkvbuf[...]
    ffn_up(*FF_CHUNKS[0])
    ffn_up(*FF_CHUNKS[1])

    ang = invf_ref[...] * pos_ref[0].astype(F32)
    cos_t, sin_t = jnp.cos(ang), jnp.sin(ang)
    cos = jnp.concatenate([cos_t] * 4, axis=0).T
    sin = jnp.concatenate([-sin_t, sin_t, -sin_t, sin_t], axis=0).T

    lane = lax.broadcasted_iota(jnp.int32, (tm, LANES), 1)
    first_half = (lane & (HEAD_DIM // 2)) == 0
    low_head = lane < HEAD_DIM

    def rope(v, c, s):
        partner = jnp.where(first_half, pltpu.roll(v, LANES - HEAD_DIM // 2, 1),
                            pltpu.roll(v, HEAD_DIM // 2, 1))
        return v * c + partner * s

    scale = HEAD_DIM ** -0.5
    cos_q, sin_q = cos * scale, sin * scale
    for c in range(Q_WIDTH // LANES):
        sl = slice(c * LANES, (c + 1) * LANES)
        qbuf[:, sl] = rope(qkv[:, sl], cos_q, sin_q).astype(BF16)

    k = rope(qkv[:, Q_WIDTH:Q_WIDTH + KV_WIDTH], cos, sin)
    krot = pltpu.roll(k, HEAD_DIM, 1)
    kzero = jnp.zeros_like(k)
    rows = slice(BLOCK, BLOCK + tm)
    kbuf[0, rows, :] = jnp.where(low_head, k, kzero).astype(BF16)
    kbuf[1, rows, :] = jnp.where(low_head, kzero, krot).astype(BF16)
    kbuf[2, rows, :] = jnp.where(low_head, krot, kzero).astype(BF16)
    kbuf[3, rows, :] = jnp.where(low_head, kzero, k).astype(BF16)
    vbuf[:, BLOCK:BLOCK + tm] = qkv[:, Q_WIDTH + KV_WIDTH:].T.astype(BF16)

    key2 = lax.broadcasted_iota(jnp.int32, (2 * BLOCK, 2 * BLOCK), 0)
    qry2 = lax.broadcasted_iota(jnp.int32, (2 * BLOCK, 2 * BLOCK), 1) & (BLOCK - 1)
    band = (key2 > qry2) & (key2 <= qry2 + BLOCK)
    neg_inf = jnp.full((2 * BLOCK, 2 * BLOCK), -jnp.inf, F32)
    zeros2 = jnp.zeros((2 * BLOCK, 2 * BLOCK), F32)
    bias = jnp.where(band, zeros2, neg_inf)
    bias_seq_start = jnp.where(band & (key2 >= BLOCK), zeros2, neg_inf)
    bias_first = jnp.where(mixer_seq_start, bias_seq_start, bias)

    z = proj(OFF_CC, D_MODEL) * proj(OFF_CX, D_MODEL)

    blocks = [(i, g) for i in range(tm // BLOCK) for g in range(N_KV_HEADS)]
    scores = {}
    for i, g in blocks:
        qrows = slice(i * BLOCK, (i + 1) * BLOCK)
        krows = slice(i * BLOCK, (i + 2) * BLOCK)
        blk_bias = bias_first if i == 0 else bias
        c0 = g * 2 * LANES
        qpair = jnp.concatenate([qbuf[qrows, c0:c0 + LANES],
                                 qbuf[qrows, c0 + LANES:c0 + 2 * LANES]], axis=0)
        scores[i, g] = (_dot_nt(kbuf[2 * g, krows, :], qpair) + blk_bias,
                        _dot_nt(kbuf[2 * g + 1, krows, :], qpair) + blk_bias)

    ffn_up(*FF_CHUNKS[2])

    top_pair = lax.broadcasted_iota(jnp.int32, (1, 2 * BLOCK), 1) < BLOCK
    sum_row = lax.broadcasted_iota(jnp.int32, (2 * HALO, 4 * BLOCK), 0)
    sum_col = lax.broadcasted_iota(jnp.int32, (2 * HALO, 4 * BLOCK), 1)
    sum_rows = jnp.where(sum_row == (sum_col >= 2 * BLOCK).astype(jnp.int32), 1.0, 0.0)
    sum_rows = sum_rows.astype(BF16)
    vzero = jnp.zeros((HEAD_DIM, 2 * BLOCK), BF16)

    for i, g in blocks:
        qrows = slice(i * BLOCK, (i + 1) * BLOCK)
        kcols = slice(i * BLOCK, (i + 2) * BLOCK)
        c0 = g * 2 * LANES
        s_even, s_odd = scores[i, g]
        sink_even = jnp.where(top_pair, sinks_ref[4 * g], sinks_ref[4 * g + 2])
        sink_odd = jnp.where(top_pair, sinks_ref[4 * g + 1], sinks_ref[4 * g + 3])
        m_even = jnp.maximum(jnp.max(s_even, axis=0, keepdims=True), sink_even)
        m_odd = jnp.maximum(jnp.max(s_odd, axis=0, keepdims=True), sink_odd)
        p_t = jnp.concatenate([jnp.exp(s_even - m_even).astype(BF16),
                               jnp.exp(s_odd - m_odd).astype(BF16)], axis=0)
        v_t = vbuf[g * HEAD_DIM:(g + 1) * HEAD_DIM, kcols]
        lhs = jnp.concatenate([jnp.concatenate([v_t, vzero], axis=1),
                               jnp.concatenate([vzero, v_t], axis=1), sum_rows], axis=0)
        r = _dot(lhs, p_t)
        inv_even = 1.0 / (r[2 * HEAD_DIM:2 * HEAD_DIM + 1] + jnp.exp(sink_even - m_even))
        inv_odd = 1.0 / (r[2 * HEAD_DIM + 1:2 * HEAD_DIM + 2] + jnp.exp(sink_odd - m_odd))
        out = jnp.concatenate([r[0:HEAD_DIM] * inv_even,
                               r[HEAD_DIM:2 * HEAD_DIM] * inv_odd], axis=0).T.astype(BF16)
        abuf[qrows, c0:c0 + LANES] = out[:BLOCK]
        abuf[qrows, c0 + LANES:c0 + 2 * LANES] = out[BLOCK:]

    kbuf[:, 0:BLOCK, :] = kbuf[:, tm:tm + BLOCK, :]
    vbuf[:, 0:BLOCK] = vbuf[:, tm:tm + BLOCK]

    split = FF_CHUNKS[1][1]
    y_down = _dot(hbuf[:, 0:split], wdn_ref[0:split, :])

    conv = _causal_conv3(zbuf, z, cmw_ref, 0, D_MODEL)
    gated = (proj(OFF_CB, D_MODEL) * conv).astype(BF16)
    gate_conv = proj(OFF_GC, D_MODEL)
    gate_attn = proj(OFF_GA, D_MODEL)
    y_conv = _dot(gated, wco_ref[...])

    y_attn = _dot(abuf[...], wao_ref[...]) + bao_ref[...]

    y = x_prev + y_down + _dot(hbuf[:, split:], wdn_ref[split:, :])
    o_ref[0] = _rmsnorm(y, fin_ref[...]) if final_norm else y

    merged = jax.nn.sigmoid(gate_conv) * y_conv + jax.nn.sigmoid(gate_attn) * y_attn
    x1 = x + _dot(merged.astype(BF16), wmo_ref[...])
    x1buf[...] = x1
    u1buf[...] = _rmsnorm(x1, nfw_ref[...]).astype(BF16)

    u_next = _rmsnorm(xnext_ref[0], nmw_ref[...]).astype(BF16)
    umbuf[...] = u_next
    qkvbuf[...] = proj_of(u_next, OFF_Q, Q_WIDTH + 2 * KV_WIDTH)


def _whole(space=pltpu.VMEM):
    return pl.BlockSpec(memory_space=space)


def kernel(x, positions, norm_mix_w, w_in, b_in, conv_mix_w, w_conv_out, w_attn_out, b_attn_out,
           sinks, w_mix_out, norm_ffn_w, w_ffn_up, ffn_conv_w, ffn_conv_b, w_ffn_down, norm_final_w):
    b, s, d = x.shape
    depth = w_in.shape[0]
    tm = TOKEN_TILE
    assert d == D_MODEL and s % tm == 0 and tm % BLOCK == 0
    nj = s // tm
    n_tiles = b * nj

    def in_tile(t):
        return jnp.minimum(t, n_tiles - 1)

    x_spec = pl.BlockSpec((1, tm, D_MODEL), lambda t: (in_tile(t), 0, 0))
    xnext_spec = pl.BlockSpec((1, tm, D_MODEL), lambda t: (in_tile(t + 1), 0, 0))
    pos_spec = pl.BlockSpec((1, 1, tm), lambda t: (in_tile(t), 0, 0))
    out_spec = pl.BlockSpec((1, tm, D_MODEL), lambda t: (jnp.maximum(t - 1, 0), 0, 0))

    inv_freq = (ROPE_THETA ** (-jnp.arange(0, HEAD_DIM, 2, dtype=F32) / HEAD_DIM)).reshape(-1, 1)
    pos_tiles = positions.reshape(n_tiles, 1, tm)

    def rows(v):
        v = v.reshape(-1, v.shape[-1])
        return jnp.pad(v, ((0, 0), (0, IN_WIDTH - v.shape[-1])))

    def layer(final_norm):
        return pl.pallas_call(
            functools.partial(_block_kernel, nj, final_norm),
            grid=(n_tiles + 1,),
            in_specs=([_whole(pltpu.SMEM), pos_spec, x_spec, xnext_spec] + [_whole()] * 2
                      + [pl.BlockSpec(memory_space=pl.ANY)] * 6),
            out_specs=out_spec,
            out_shape=jax.ShapeDtypeStruct((n_tiles, tm, D_MODEL), F32),
            scratch_shapes=[
                pltpu.VMEM((HALO, D_MODEL), F32),
                pltpu.VMEM((4, BLOCK + tm, LANES), BF16),
                pltpu.VMEM((KV_WIDTH, BLOCK + tm), BF16),
                pltpu.VMEM((tm, Q_WIDTH), BF16),
                pltpu.VMEM((tm, Q_WIDTH), BF16),
                pltpu.VMEM((tm, D_MODEL), F32),
                pltpu.VMEM((tm, D_MODEL), BF16),
                pltpu.VMEM((tm, D_MODEL), BF16),
                pltpu.VMEM((tm, Q_WIDTH + 2 * KV_WIDTH), F32),
                pltpu.VMEM((HALO, D_FF), F32),
                pltpu.VMEM((tm, D_FF), BF16),
                pltpu.VMEM((D_MODEL, IN_WIDTH), BF16),
                pltpu.VMEM((D_MODEL, 2 * D_FF), BF16),
                pltpu.VMEM((D_MODEL, D_MODEL), BF16),
                pltpu.VMEM((D_MODEL, D_MODEL), BF16),
                pltpu.VMEM((Q_WIDTH, D_MODEL), BF16),
                pltpu.VMEM((D_FF, D_MODEL), BF16),
                pltpu.VMEM((STAGE_SLOTS, STAGE_ROWS_WIDE, IN_WIDTH), F32),
                pltpu.VMEM((STAGE_SLOTS, STAGE_ROWS_NARROW, D_MODEL), F32),
                pltpu.SemaphoreType.DMA((STAGE_SLOTS,)),
            ],
            compiler_params=pltpu.CompilerParams(dimension_semantics=("arbitrary",),
                                                 vmem_limit_bytes=VMEM_LIMIT_BYTES),
            name="decoder_block",
        )

    x = x.reshape(n_tiles, tm, D_MODEL)
    for l in range(depth):
        params = jnp.concatenate(
            [rows(b_in[l]), rows(norm_mix_w[l]), rows(b_attn_out[l]), rows(norm_ffn_w[l]),
             rows(norm_final_w), rows(conv_mix_w[l]), rows(ffn_conv_w[l]), rows(ffn_conv_b[l])],
            axis=0)
        x = layer(l == depth - 1)(
            sinks[l], pos_tiles, x, x, inv_freq, params, w_in[l], w_ffn_up[l], w_conv_out[l],
            w_mix_out[l], w_attn_out[l], w_ffn_down[l])
    return x.reshape(b, s, D_MODEL)
```

```python
import functools

import jax
import jax.numpy as jnp
from jax import lax
from jax.experimental import pallas as pl
from jax.experimental.pallas import tpu as pltpu

D_MODEL = 1024
HEAD_DIM = 64
N_Q_HEADS = 8
N_KV_HEADS = 2
BLOCK = 128
ROPE_THETA = 10000.0
Q_WIDTH = N_Q_HEADS * HEAD_DIM
KV_WIDTH = N_KV_HEADS * HEAD_DIM
D_FF = 2816
EPS = 1e-5

OFF_CB = 0
OFF_CC = OFF_CB + D_MODEL
OFF_CX = OFF_CC + D_MODEL
OFF_Q = OFF_CX + D_MODEL
OFF_K = OFF_Q + Q_WIDTH
OFF_V = OFF_K + KV_WIDTH
OFF_GC = OFF_V + KV_WIDTH
OFF_GA = OFF_GC + D_MODEL
IN_WIDTH = OFF_GA + D_MODEL

ROW_B_IN, ROW_NORM_MIX, ROW_B_ATTN_OUT, ROW_NORM_FFN, ROW_NORM_FINAL = 0, 1, 2, 3, 4
ROW_CONV_MIX, ROW_FFN_CONV, ROW_FFN_CONV_B = 5, 8, 11

LANES = 128
HALO = 8
TOKEN_TILE = 256
FF_CHUNKS = ((0, 1024), (1024, 2048), (2048, D_FF))
VMEM_LIMIT_BYTES = 60 * 1024 * 1024

STAGE_SLOTS = 4
STAGE_ROWS_WIDE = 32
STAGE_ROWS_NARROW = 128

F32 = jnp.float32
BF16 = jnp.bfloat16


def _rmsnorm(x, w):
    ms = jnp.mean(x * x, axis=-1, keepdims=True)
    return x * lax.rsqrt(ms + EPS) * w


def _dot(a, b):
    return jnp.dot(a, b, preferred_element_type=F32)


def _dot_nt(a, b):
    return lax.dot_general(a, b, (((1,), (1,)), ((), ())), preferred_element_type=F32)


def _causal_conv3(halo_ref, cur, taps_ref, c0, c1):
    tm = cur.shape[0]
    first_row = lax.broadcasted_iota(jnp.int32, (HALO, c1 - c0), 0) == 0

    def shift_down(v, halo_row):
        rolled = pltpu.roll(v, 1, 0)
        head = jnp.where(first_row, halo_row, rolled[0:HALO])
        return jnp.concatenate([head, rolled[HALO:]], axis=0)

    prev1 = shift_down(cur, halo_ref[HALO - 1:HALO, c0:c1])
    prev2 = shift_down(prev1, halo_ref[HALO - 2:HALO - 1, c0:c1])
    halo_ref[HALO - 2:HALO, c0:c1] = cur[tm - 2:tm]
    return (taps_ref[0:1, c0:c1] * prev2 + taps_ref[1:2, c0:c1] * prev1
            + taps_ref[2:3, c0:c1] * cur)


def _load_as_bf16(src_hbm, dst_ref, stage, sems):
    rows, width = src_hbm.shape
    chunk = stage.shape[1]
    assert rows % chunk == 0 and width <= stage.shape[2]
    n = rows // chunk
    ahead = STAGE_SLOTS - 1

    def copy(i, slot):
        r0 = i * chunk if isinstance(i, int) else pl.multiple_of(i * chunk, chunk)
        return pltpu.make_async_copy(src_hbm.at[pl.ds(r0, chunk), :],
                                     stage.at[slot, :, 0:width], sems.at[slot])

    for i in range(min(ahead, n)):
        copy(i, i).start()

    def body(i, carry):
        slot = lax.rem(i, STAGE_SLOTS)
        copy(i, slot).wait()

        @pl.when(i + ahead < n)
        def _():
            copy(i + ahead, lax.rem(i + ahead, STAGE_SLOTS)).start()

        r0 = pl.multiple_of(i * chunk, chunk)
        dst_ref[pl.ds(r0, chunk), :] = stage[slot, :, 0:width].astype(BF16)
        return carry

    lax.fori_loop(0, n, body, 0)


def _block_kernel(tiles_per_seq, final_norm,
                  sinks_ref, pos_ref, x_ref, xnext_ref, invf_ref, prm_ref, win_hbm, wup_hbm,
                  wco_hbm, wmo_hbm, wao_hbm, wdn_hbm, o_ref,
                  zbuf, kbuf, vbuf, qbuf, abuf, x1buf, u1buf, umbuf, qkvbuf, ubuf, hbuf,
                  win_ref, wup_ref, wco_ref, wmo_ref, wao_ref, wdn_ref,
                  stage_wide, stage_narrow, stage_sems):
    t = pl.program_id(0)
    tm = x_ref.shape[1]
    mixer_seq_start = lax.rem(t, tiles_per_seq) == 0
    ffn_seq_start = lax.rem(t + tiles_per_seq - 1, tiles_per_seq) == 0

    bin_ref = prm_ref.at[ROW_B_IN:ROW_B_IN + 1, :]
    nmw_ref = prm_ref.at[ROW_NORM_MIX:ROW_NORM_MIX + 1, 0:D_MODEL]
    bao_ref = prm_ref.at[ROW_B_ATTN_OUT:ROW_B_ATTN_OUT + 1, 0:D_MODEL]
    nfw_ref = prm_ref.at[ROW_NORM_FFN:ROW_NORM_FFN + 1, 0:D_MODEL]
    fin_ref = prm_ref.at[ROW_NORM_FINAL:ROW_NORM_FINAL + 1, 0:D_MODEL]
    cmw_ref = prm_ref.at[ROW_CONV_MIX:ROW_CONV_MIX + 3, 0:D_MODEL]
    fcw_ref = prm_ref.at[ROW_FFN_CONV:ROW_FFN_CONV + 3, 0:D_FF]
    fcb_ref = prm_ref.at[ROW_FFN_CONV_B:ROW_FFN_CONV_B + 1, 0:D_FF]

    @pl.when(t == 0)
    def _():
        x1buf[...] = jnp.zeros(x1buf.shape, F32)
        u1buf[...] = jnp.zeros(u1buf.shape, BF16)
        _load_as_bf16(win_hbm, win_ref, stage_wide, stage_sems)
        _load_as_bf16(wup_hbm, wup_ref, stage_wide, stage_sems)
        _load_as_bf16(wco_hbm, wco_ref, stage_narrow, stage_sems)
        _load_as_bf16(wmo_hbm, wmo_ref, stage_narrow, stage_sems)
        _load_as_bf16(wao_hbm, wao_ref, stage_narrow, stage_sems)
        _load_as_bf16(wdn_hbm, wdn_ref, stage_narrow, stage_sems)
        u_first = _rmsnorm(x_ref[0], nmw_ref[...]).astype(BF16)
        umbuf[...] = u_first
        qkvbuf[...] = (_dot(u_first, win_ref[:, OFF_Q:OFF_GC]) + bin_ref[:, OFF_Q:OFF_GC])

    @pl.when(mixer_seq_start)
    def _():
        zbuf[0:HALO, :] = jnp.zeros((HALO, D_MODEL), F32)
        kbuf[:, 0:BLOCK, :] = jnp.zeros((4, BLOCK, LANES), BF16)
        vbuf[:, 0:BLOCK] = jnp.zeros((KV_WIDTH, BLOCK), BF16)

    @pl.when(jnp.logical_or(ffn_seq_start, t == 0))
    def _():
        ubuf[0:HALO, :] = jnp.zeros((HALO, D_FF), F32)

    x_prev = x1buf[...]
    u_prev = u1buf[...]
    x = x_ref[0]
    u = umbuf[...]

    def proj_of(lhs, off, width):
        return _dot(lhs, win_ref[:, off:off + width]) + bin_ref[:, off:off + width]

    def proj(off, width):
        return proj_of(u, off, width)

    def ffn_up(c0, c1):
        up = _dot(u_prev, wup_ref[:, c0:c1])
        a = _causal_conv3(ubuf, up, fcw_ref, c0, c1) + fcb_ref[:, c0:c1]
        gate = _dot(u_prev, wup_ref[:, D_FF + c0:D_FF + c1])
        a16 = a.astype(BF16)
        hbuf[:, c0:c1] = a16 * jax.nn.sigmoid(a16) * gate.astype(BF16)

    qkv = qkvbuf[...]
    ffn_up(*FF_CHUNKS[0])
    ffn_up(*FF_CHUNKS[1])

    ang = invf_ref[...] * pos_ref[0].astype(F32)
    cos_t, sin_t = jnp.cos(ang), jnp.sin(ang)
    cos = jnp.concatenate([cos_t] * 4, axis=0).T
    sin = jnp.concatenate([-sin_t, sin_t, -sin_t, sin_t], axis=0).T

    lane = lax.broadcasted_iota(jnp.int32, (tm, LANES), 1)
    first_half = (lane & (HEAD_DIM // 2)) == 0
    low_head = lane < HEAD_DIM

    def rope(v, c, s):
        partner = jnp.where(first_half, pltpu.roll(v, LANES - HEAD_DIM // 2, 1),
                            pltpu.roll(v, HEAD_DIM // 2, 1))
        return v * c + partner * s

    scale = HEAD_DIM ** -0.5
    cos_q, sin_q = cos * scale, sin * scale
    for c in range(Q_WIDTH // LANES):
        sl = slice(c * LANES, (c + 1) * LANES)
        qbuf[:, sl] = rope(qkv[:, sl], cos_q, sin_q).astype(BF16)

    k = rope(qkv[:, Q_WIDTH:Q_WIDTH + KV_WIDTH], cos, sin)
    krot = pltpu.roll(k, HEAD_DIM, 1)
    kzero = jnp.zeros_like(k)
    rows = slice(BLOCK, BLOCK + tm)
    kbuf[0, rows, :] = jnp.where(low_head, k, kzero).astype(BF16)
    kbuf[1, rows, :] = jnp.where(low_head, kzero, krot).astype(BF16)
    kbuf[2, rows, :] = jnp.where(low_head, krot, kzero).astype(BF16)
    kbuf[3, rows, :] = jnp.where(low_head, kzero, k).astype(BF16)
    vbuf[:, BLOCK:BLOCK + tm] = qkv[:, Q_WIDTH + KV_WIDTH:].T.astype(BF16)

    key2 = lax.broadcasted_iota(jnp.int32, (2 * BLOCK, 2 * BLOCK), 0)
    qry2 = lax.broadcasted_iota(jnp.int32, (2 * BLOCK, 2 * BLOCK), 1) & (BLOCK - 1)
    band = (key2 > qry2) & (key2 <= qry2 + BLOCK)
    neg_inf = jnp.full((2 * BLOCK, 2 * BLOCK), -jnp.inf, F32)
    zeros2 = jnp.zeros((2 * BLOCK, 2 * BLOCK), F32)
    bias = jnp.where(band, zeros2, neg_inf)
    bias_seq_start = jnp.where(band & (key2 >= BLOCK), zeros2, neg_inf)
    bias_first = jnp.where(mixer_seq_start, bias_seq_start, bias)

    z = proj(OFF_CC, D_MODEL) * proj(OFF_CX, D_MODEL)

    blocks = [(i, g) for i in range(tm // BLOCK) for g in range(N_KV_HEADS)]
    scores = {}
    for i, g in blocks:
        qrows = slice(i * BLOCK, (i + 1) * BLOCK)
        krows = slice(i * BLOCK, (i + 2) * BLOCK)
        blk_bias = bias_first if i == 0 else bias
        c0 = g * 2 * LANES
        qpair = jnp.concatenate([qbuf[qrows, c0:c0 + LANES],
                                 qbuf[qrows, c0 + LANES:c0 + 2 * LANES]], axis=0)
        scores[i, g] = (_dot_nt(kbuf[2 * g, krows, :], qpair) + blk_bias,
                        _dot_nt(kbuf[2 * g + 1, krows, :], qpair) + blk_bias)

    ffn_up(*FF_CHUNKS[2])

    top_pair = lax.broadcasted_iota(jnp.int32, (1, 2 * BLOCK), 1) < BLOCK
    sum_row = lax.broadcasted_iota(jnp.int32, (2 * HALO, 4 * BLOCK), 0)
    sum_col = lax.broadcasted_iota(jnp.int32, (2 * HALO, 4 * BLOCK), 1)
    sum_rows = jnp.where(sum_row == (sum_col >= 2 * BLOCK).astype(jnp.int32), 1.0, 0.0)
    sum_rows = sum_rows.astype(BF16)
    vzero = jnp.zeros((HEAD_DIM, 2 * BLOCK), BF16)

    for i, g in blocks:
        qrows = slice(i * BLOCK, (i + 1) * BLOCK)
        kcols = slice(i * BLOCK, (i + 2) * BLOCK)
        c0 = g * 2 * LANES
        s_even, s_odd = scores[i, g]
        sink_even = jnp.where(top_pair, sinks_ref[4 * g], sinks_ref[4 * g + 2])
        sink_odd = jnp.where(top_pair, sinks_ref[4 * g + 1], sinks_ref[4 * g + 3])
        m_even = jnp.maximum(jnp.max(s_even, axis=0, keepdims=True), sink_even)
        m_odd = jnp.maximum(jnp.max(s_odd, axis=0, keepdims=True), sink_odd)
        p_t = jnp.concatenate([jnp.exp(s_even - m_even).astype(BF16),
                               jnp.exp(s_odd - m_odd).astype(BF16)], axis=0)
        v_t = vbuf[g * HEAD_DIM:(g + 1) * HEAD_DIM, kcols]
        lhs = jnp.concatenate([jnp.concatenate([v_t, vzero], axis=1),
                               jnp.concatenate([vzero, v_t], axis=1), sum_rows], axis=0)
        r = _dot(lhs, p_t)
        inv_even = 1.0 / (r[2 * HEAD_DIM:2 * HEAD_DIM + 1] + jnp.exp(sink_even - m_even))
        inv_odd = 1.0 / (r[2 * HEAD_DIM + 1:2 * HEAD_DIM + 2] + jnp.exp(sink_odd - m_odd))
        out = jnp.concatenate([r[0:HEAD_DIM] * inv_even,
                               r[HEAD_DIM:2 * HEAD_DIM] * inv_odd], axis=0).T.astype(BF16)
        abuf[qrows, c0:c0 + LANES] = out[:BLOCK]
        abuf[qrows, c0 + LANES:c0 + 2 * LANES] = out[BLOCK:]

    kbuf[:, 0:BLOCK, :] = kbuf[:, tm:tm + BLOCK, :]
    vbuf[:, 0:BLOCK] = vbuf[:, tm:tm + BLOCK]

    conv = _causal_conv3(zbuf, z, cmw_ref, 0, D_MODEL)
    gated = (proj(OFF_CB, D_MODEL) * conv).astype(BF16)
    gate_conv = proj(OFF_GC, D_MODEL)
    gate_attn = proj(OFF_GA, D_MODEL)
    y_conv = _dot(gated, wco_ref[...])

    y_attn = _dot(abuf[...], wao_ref[...]) + bao_ref[...]

    y = x_prev + _dot(hbuf[...], wdn_ref[...])
    o_ref[0] = _rmsnorm(y, fin_ref[...]) if final_norm else y

    merged = jax.nn.sigmoid(gate_conv) * y_conv + jax.nn.sigmoid(gate_attn) * y_attn
    x1 = x + _dot(merged.astype(BF16), wmo_ref[...])
    x1buf[...] = x1
    u1buf[...] = _rmsnorm(x1, nfw_ref[...]).astype(BF16)

    u_next = _rmsnorm(xnext_ref[0], nmw_ref[...]).astype(BF16)
    umbuf[...] = u_next
    qkvbuf[...] = proj_of(u_next, OFF_Q, Q_WIDTH + 2 * KV_WIDTH)


def _whole(space=pltpu.VMEM):
    return pl.BlockSpec(memory_space=space)


def kernel(x, positions, norm_mix_w, w_in, b_in, conv_mix_w, w_conv_out, w_attn_out, b_attn_out,
           sinks, w_mix_out, norm_ffn_w, w_ffn_up, ffn_conv_w, ffn_conv_b, w_ffn_down, norm_final_w):
    b, s, d = x.shape
    depth = w_in.shape[0]
    tm = TOKEN_TILE
    assert d == D_MODEL and s % tm == 0 and tm % BLOCK == 0
    nj = s // tm
    n_tiles = b * nj

    def in_tile(t):
        return jnp.minimum(t, n_tiles - 1)

    x_spec = pl.BlockSpec((1, tm, D_MODEL), lambda t: (in_tile(t), 0, 0))
    xnext_spec = pl.BlockSpec((1, tm, D_MODEL), lambda t: (in_tile(t + 1), 0, 0))
    pos_spec = pl.BlockSpec((1, 1, tm), lambda t: (in_tile(t), 0, 0))
    out_spec = pl.BlockSpec((1, tm, D_MODEL), lambda t: (jnp.maximum(t - 1, 0), 0, 0))

    inv_freq = (ROPE_THETA ** (-jnp.arange(0, HEAD_DIM, 2, dtype=F32) / HEAD_DIM)).reshape(-1, 1)
    pos_tiles = positions.reshape(n_tiles, 1, tm)

    def rows(v):
        v = v.reshape(-1, v.shape[-1])
        return jnp.pad(v, ((0, 0), (0, IN_WIDTH - v.shape[-1])))

    def layer(final_norm):
        return pl.pallas_call(
            functools.partial(_block_kernel, nj, final_norm),
            grid=(n_tiles + 1,),
            in_specs=([_whole(pltpu.SMEM), pos_spec, x_spec, xnext_spec] + [_whole()] * 2
                      + [pl.BlockSpec(memory_space=pl.ANY)] * 6),
            out_specs=out_spec,
            out_shape=jax.ShapeDtypeStruct((n_tiles, tm, D_MODEL), F32),
            scratch_shapes=[
                pltpu.VMEM((HALO, D_MODEL), F32),
                pltpu.VMEM((4, BLOCK + tm, LANES), BF16),
                pltpu.VMEM((KV_WIDTH, BLOCK + tm), BF16),
                pltpu.VMEM((tm, Q_WIDTH), BF16),
                pltpu.VMEM((tm, Q_WIDTH), BF16),
                pltpu.VMEM((tm, D_MODEL), F32),
                pltpu.VMEM((tm, D_MODEL), BF16),
                pltpu.VMEM((tm, D_MODEL), BF16),
                pltpu.VMEM((tm, Q_WIDTH + 2 * KV_WIDTH), F32),
                pltpu.VMEM((HALO, D_FF), F32),
                pltpu.VMEM((tm, D_FF), BF16),
                pltpu.VMEM((D_MODEL, IN_WIDTH), BF16),
                pltpu.VMEM((D_MODEL, 2 * D_FF), BF16),
                pltpu.VMEM((D_MODEL, D_MODEL), BF16),
                pltpu.VMEM((D_MODEL, D_MODEL), BF16),
                pltpu.VMEM((Q_WIDTH, D_MODEL), BF16),
                pltpu.VMEM((D_FF, D_MODEL), BF16),
                pltpu.VMEM((STAGE_SLOTS, STAGE_ROWS_WIDE, IN_WIDTH), F32),
                pltpu.VMEM((STAGE_SLOTS, STAGE_ROWS_NARROW, D_MODEL), F32),
                pltpu.SemaphoreType.DMA((STAGE_SLOTS,)),
            ],
            compiler_params=pltpu.CompilerParams(dimension_semantics=("arbitrary",),
                                                 vmem_limit_bytes=VMEM_LIMIT_BYTES),
            name="decoder_block",
        )

    x = x.reshape(n_tiles, tm, D_MODEL)
    for l in range(depth):
        params = jnp.concatenate(
            [rows(b_in[l]), rows(norm_mix_w[l]), rows(b_attn_out[l]), rows(norm_ffn_w[l]),
             rows(norm_final_w), rows(conv_mix_w[l]), rows(ffn_conv_w[l]), rows(ffn_conv_b[l])],
            axis=0)
        x = layer(l == depth - 1)(
            sinks[l], pos_tiles, x, x, inv_freq, params, w_in[l], w_ffn_up[l], w_conv_out[l],
            w_mix_out[l], w_attn_out[l], w_ffn_down[l])
    return x.reshape(b, s, D_MODEL)
```
